```python
import jax, jax.numpy as jnp
from jax import lax
import numpy as np

D_MODEL = 1024
BATCH = 4
SEQ = 4096
DEPTH = 4

N_MIXERS = 3
HEAD_DIM = 64
MIX_HEADS = 12
MIX_WIDTH = MIX_HEADS * HEAD_DIM
MEM_LEN = 256
MEM_HEADS = 4
MEM_WIDTH = MEM_HEADS * HEAD_DIM
CAT_WIDTH = MIX_WIDTH + MEM_WIDTH
KV_HEADS = 4
KV_WIDTH = KV_HEADS * HEAD_DIM
ROT_DIM = HEAD_DIM // 4
ROPE_THETA = 500000.0
CONV_WIDTH = 3
Q_RANK = 256
IDX_HEADS = 8
IDX_DIM = 64
TOPK_MAX = 256
BLOCK = 128
WINDOW = 128
N_EXPERTS = 32
TOP_K = 4
D_EXPERT = 1024
SWIGLU_LIMIT = 7.0
SWIGLU_ALPHA = 1.702
ALPHA = (2 * DEPTH) ** 0.25
BETA = (8 * DEPTH) ** -0.25
LN_EPS = 1e-5
N_A = (DEPTH + 2) // 3
N_B = (DEPTH + 1) // 3
N_C = DEPTH // 3
A_IN = 3 * MIX_WIDTH + MEM_WIDTH
B_IN = Q_RANK + 2 * KV_WIDTH + IDX_DIM + IDX_HEADS + MEM_WIDTH
C_IN = MIX_WIDTH + 2 * KV_WIDTH + MEM_WIDTH

kernel_name = 'hybrid_conv_dsa_swa_memxattn_moe_trunk'


def layer_norm(x, g, b):
    xf = x.astype(jnp.float32)
    mu = xf.mean(-1, keepdims=True)
    var = jnp.square(xf - mu).mean(-1, keepdims=True)
    return ((xf - mu) * lax.rsqrt(var + LN_EPS) * g.astype(jnp.float32) + b.astype(jnp.float32)).astype(x.dtype)


def rms_norm(x, g):
    xf = x.astype(jnp.float32)
    y = xf * lax.rsqrt(jnp.mean(jnp.square(xf), -1, keepdims=True) + LN_EPS)
    return (y * g.astype(jnp.float32)).astype(x.dtype)


def rope_tables(seq_len):
    inv = ROPE_THETA ** (-jnp.arange(0, ROT_DIM, 2, dtype=jnp.float32) / ROT_DIM)
    ang = jnp.arange(seq_len, dtype=jnp.float32)[:, None] * inv[None, :]
    return jnp.cos(ang), jnp.sin(ang)


def partial_rope(x, cos, sin):
    c = cos[None, :, None, :].astype(x.dtype)
    s = sin[None, :, None, :].astype(x.dtype)
    x1 = x[..., :ROT_DIM // 2]
    x2 = x[..., ROT_DIM // 2:ROT_DIM]
    return jnp.concatenate([x1 * c - x2 * s, x2 * c + x1 * s, x[..., ROT_DIM:]], -1)


def memory_attention(qm, mem, w_kv_mem):
    B, S, _ = qm.shape
    M = mem.shape[1]
    q = qm.reshape(B, S, MEM_HEADS, HEAD_DIM)
    k, v = jnp.split(mem @ w_kv_mem, 2, -1)
    k = k.reshape(B, M, MEM_HEADS, HEAD_DIM)
    v = v.reshape(B, M, MEM_HEADS, HEAD_DIM)
    logits = jnp.einsum('bshd,bmhd->bhsm', q, k).astype(jnp.float32) * HEAD_DIM ** -0.5
    p = jax.nn.softmax(logits, -1).astype(v.dtype)
    return jnp.einsum('bhsm,bmhd->bshd', p, v).reshape(B, S, MEM_WIDTH)


def short_conv_mixer(x, w_in, conv_w):
    S = x.shape[1]
    u, b_gate, c_gate, qm = jnp.split(x @ w_in, [MIX_WIDTH, 2 * MIX_WIDTH, 3 * MIX_WIDTH], -1)
    zp = jnp.pad(c_gate * u, ((0, 0), (CONV_WIDTH - 1, 0), (0, 0)))
    conv = conv_w[0] * zp[:, 0:S]
    for tap in range(1, CONV_WIDTH):
        conv = conv + conv_w[tap] * zp[:, tap:tap + S]
    return b_gate * conv, qm


def dsa_mixer(x, w_in, q_norm_g, w_uq, w_uq_idx, cos, sin):
    B, S, _ = x.shape
    o1 = Q_RANK
    o2 = o1 + KV_WIDTH
    o3 = o2 + KV_WIDTH
    o4 = o3 + IDX_DIM
    o5 = o4 + IDX_HEADS
    cq, k, v, k_idx, w_idx, qm = jnp.split(x @ w_in, [o1, o2, o3, o4, o5], -1)
    cq = rms_norm(cq, q_norm_g)
    q = partial_rope((cq @ w_uq).reshape(B, S, MIX_HEADS, HEAD_DIM), cos, sin)
    q_idx = partial_rope((cq @ w_uq_idx).reshape(B, S, IDX_HEADS, IDX_DIM), cos, sin)
    k = partial_rope(k.reshape(B, S, KV_HEADS, HEAD_DIM), cos, sin)
    v = v.reshape(B, S, KV_HEADS, HEAD_DIM)
    k_idx = partial_rope(k_idx[:, :, None, :], cos, sin)[:, :, 0]
    w_idx = w_idx.astype(jnp.float32) * IDX_HEADS ** -0.5
    n_sel = min(TOPK_MAX, S // 4)
    n_blk = S // BLOCK
    rep = MIX_HEADS // KV_HEADS
    key_pos = jnp.arange(S)

    def query_block(j):
        start = j * BLOCK
        qb = lax.dynamic_slice_in_dim(q, start, BLOCK, 1)
        qib = lax.dynamic_slice_in_dim(q_idx, start, BLOCK, 1)
        wb = lax.dynamic_slice_in_dim(w_idx, start, BLOCK, 1)
        q_pos = start + jnp.arange(BLOCK)
        dots = jnp.einsum('bqhd,bsd->bqhs', qib, k_idx).astype(jnp.float32) * IDX_DIM ** -0.5
        score = jnp.einsum('bqh,bqhs->bqs', wb, jax.nn.relu(dots))
        causal = key_pos[None, :] <= q_pos[:, None]
        score = jnp.where(causal[None], score, -jnp.inf)
        _, sel = lax.top_k(score, n_sel)
        valid = sel <= q_pos[None, :, None]
        kg = jax.vmap(lambda kb, ib: kb[ib])(k, sel)
        vg = jax.vmap(lambda vb, ib: vb[ib])(v, sel)
        qb = qb.reshape(B, BLOCK, KV_HEADS, rep, HEAD_DIM)
        logits = jnp.einsum('bqgrd,bqkgd->bqgrk', qb, kg).astype(jnp.float32) * HEAD_DIM ** -0.5
        logits = jnp.where(valid[:, :, None, None, :], logits, -jnp.inf)
        p = jax.nn.softmax(logits, -1).astype(vg.dtype)
        return jnp.einsum('bqgrk,bqkgd->bqgrd', p, vg).reshape(B, BLOCK, MIX_WIDTH)

    out = lax.map(query_block, jnp.arange(n_blk))
    return out.transpose(1, 0, 2, 3).reshape(B, S, MIX_WIDTH), qm


def swa_mixer(x, w_in, sinks, cos, sin):
    B, S, _ = x.shape
    q, k, v, qm = jnp.split(x @ w_in, [MIX_WIDTH, MIX_WIDTH + KV_WIDTH, MIX_WIDTH + 2 * KV_WIDTH], -1)
    q = partial_rope(q.reshape(B, S, MIX_HEADS, HEAD_DIM), cos, sin)
    k = partial_rope(k.reshape(B, S, KV_HEADS, HEAD_DIM), cos, sin)
    v = v.reshape(B, S, KV_HEADS, HEAD_DIM)
    n_blk = S // BLOCK
    rep = MIX_HEADS // KV_HEADS
    qb = q.reshape(B, n_blk, BLOCK, KV_HEADS, rep, HEAD_DIM)

    def band(t):
        tb = jnp.pad(t, ((0, 0), (BLOCK, 0), (0, 0), (0, 0))).reshape(B, n_blk + 1, BLOCK, KV_HEADS, HEAD_DIM)
        return jnp.concatenate([tb[:, :-1], tb[:, 1:]], axis=2)

    kb, vb = band(k), band(v)
    logits = jnp.einsum('bnqgrd,bnkgd->bngrqk', qb, kb).astype(jnp.float32) * HEAD_DIM ** -0.5
    i = jnp.arange(BLOCK)[:, None]
    r = jnp.arange(2 * BLOCK)[None, :]
    blk_start = (jnp.arange(n_blk) * BLOCK)[:, None, None]
    mask = (r > i) & (r <= i + WINDOW) & (blk_start + r >= BLOCK)
    logits = jnp.where(mask[None, :, None, None], logits, -jnp.inf)
    sink = sinks.astype(jnp.float32).reshape(KV_HEADS, rep)[None, None, :, :, None, None]
    m = jnp.maximum(logits.max(-1, keepdims=True), sink)
    e = jnp.exp(logits - m)
    p = e / (e.sum(-1, keepdims=True) + jnp.exp(sink - m))
    o = jnp.einsum('bngrqk,bnkgd->bnqgrd', p.astype(vb.dtype), vb)
    return o.reshape(B, S, MIX_WIDTH), qm


def moe(x, router_w, router_b, w_gate_up, b_gate_up, w_down, b_down):
    B, S, D = x.shape
    xt = x.reshape(B * S, D)
    logits = (xt @ router_w + router_b).astype(jnp.float32)
    top_val, top_idx = lax.top_k(logits, TOP_K)
    gates = jax.nn.softmax(top_val, -1)
    combine = jnp.einsum('tk,tke->te', gates, jax.nn.one_hot(top_idx, N_EXPERTS, dtype=jnp.float32)).astype(x.dtype)
    out = jnp.zeros_like(xt)
    for e in range(N_EXPERTS):
        h = xt @ w_gate_up[e] + b_gate_up[e]
        gate = jnp.minimum(h[:, :D_EXPERT], SWIGLU_LIMIT)
        up = jnp.clip(h[:, D_EXPERT:], -SWIGLU_LIMIT, SWIGLU_LIMIT)
        act = (up + 1.0) * gate * jax.nn.sigmoid(SWIGLU_ALPHA * gate)
        out = out + combine[:, e:e + 1] * (act @ w_down[e] + b_down[e])
    return out.reshape(B, S, D)


def setup_inputs(seed: int = 0) -> dict:
    key = jax.random.key(seed)
    ks = iter(jax.random.split(key, 32))

    def nrm(shape, scale):
        return jax.random.normal(next(ks), shape, jnp.float32) * scale

    return {
        'x': nrm((BATCH, SEQ, D_MODEL), 1.0),
        'mem': nrm((BATCH, MEM_LEN, D_MODEL), 1.0),
        'a_w_in': nrm((N_A, D_MODEL, A_IN), D_MODEL ** -0.5),
        'a_conv_w': nrm((N_A, CONV_WIDTH, MIX_WIDTH), CONV_WIDTH ** -0.5),
        'b_w_in': nrm((N_B, D_MODEL, B_IN), D_MODEL ** -0.5),
        'b_q_norm_g': 1.0 + nrm((N_B, Q_RANK), 0.02),
        'b_w_uq': nrm((N_B, Q_RANK, MIX_WIDTH), Q_RANK ** -0.5),
        'b_w_uq_idx': nrm((N_B, Q_RANK, IDX_HEADS * IDX_DIM), Q_RANK ** -0.5),
        'c_w_in': nrm((N_C, D_MODEL, C_IN), D_MODEL ** -0.5),
        'c_sinks': nrm((N_C, MIX_HEADS), 1.0),
        'w_out': nrm((DEPTH, CAT_WIDTH, D_MODEL), BETA * CAT_WIDTH ** -0.5),
        'w_kv_mem': nrm((DEPTH, D_MODEL, 2 * MEM_WIDTH), D_MODEL ** -0.5),
        'ln1_g': 1.0 + nrm((DEPTH, D_MODEL), 0.02),
        'ln1_b': nrm((DEPTH, D_MODEL), 0.02),
        'router_w': nrm((DEPTH, D_MODEL, N_EXPERTS), D_MODEL ** -0.5),
        'router_b': nrm((DEPTH, N_EXPERTS), 0.01),
        'w_gate_up': nrm((DEPTH, N_EXPERTS, D_MODEL, 2 * D_EXPERT), D_MODEL ** -0.5),
        'b_gate_up': nrm((DEPTH, N_EXPERTS, 2 * D_EXPERT), 0.01),
        'w_down': nrm((DEPTH, N_EXPERTS, D_EXPERT, D_MODEL), BETA * D_EXPERT ** -0.5),
        'b_down': nrm((DEPTH, N_EXPERTS, D_MODEL), 0.01),
        'ln2_g': 1.0 + nrm((DEPTH, D_MODEL), 0.02),
        'ln2_b': nrm((DEPTH, D_MODEL), 0.02),
    }


def reference(x, mem, a_w_in, a_conv_w, b_w_in, b_q_norm_g, b_w_uq, b_w_uq_idx, c_w_in, c_sinks,
              w_out, w_kv_mem, ln1_g, ln1_b, router_w, router_b, w_gate_up, b_gate_up, w_down, b_down,
              ln2_g, ln2_b):
    cos, sin = rope_tables(x.shape[1])
    for i in range(DEPTH):
        kind = i % N_MIXERS
        j = i // N_MIXERS
        if kind == 0:
            mix, qm = short_conv_mixer(x, a_w_in[j], a_conv_w[j])
        elif kind == 1:
            mix, qm = dsa_mixer(x, b_w_in[j], b_q_norm_g[j], b_w_uq[j], b_w_uq_idx[j], cos, sin)
        else:
            mix, qm = swa_mixer(x, c_w_in[j], c_sinks[j], cos, sin)
        mem_out = memory_attention(qm, mem, w_kv_mem[i])
        y = jnp.concatenate([mix, mem_out], -1) @ w_out[i]
        x = layer_norm(ALPHA * x + y, ln1_g[i], ln1_b[i])
        f = moe(x, router_w[i], router_b[i], w_gate_up[i], b_gate_up[i], w_down[i], b_down[i])
        x = layer_norm(ALPHA * x + f, ln2_g[i], ln2_b[i])
    return x
```

```python
import functools

import jax
import jax.numpy as jnp
from jax import lax
from jax.experimental import pallas as pl
from jax.experimental.pallas import tpu as pltpu

F32 = jnp.float32
BF16 = jnp.bfloat16
I32 = jnp.int32

D_MODEL = 1024
DEPTH = 4
N_MIXERS = 3
HEAD_DIM = 64
MIX_HEADS = 12
MIX_WIDTH = MIX_HEADS * HEAD_DIM
MEM_HEADS = 4
MEM_WIDTH = MEM_HEADS * HEAD_DIM
KV_HEADS = 4
KV_WIDTH = KV_HEADS * HEAD_DIM
HEADS_PER_KV = MIX_HEADS // KV_HEADS
ROT_DIM = HEAD_DIM // 4
ROPE_THETA = 500000.0
CONV_WIDTH = 3
Q_RANK = 256
IDX_HEADS = 8
IDX_DIM = 64
TOPK_MAX = 256
BLOCK = 128
N_EXPERTS = 32
TOP_K = 4
D_EXPERT = 1024
SWIGLU_LIMIT = 7.0
SWIGLU_ALPHA = 1.702
ALPHA = (2 * DEPTH) ** 0.25
LN_EPS = 1e-5

LANES = 128
QK_SCALE = HEAD_DIM ** -0.5
INT_MIN = -(2 ** 31)
NEG_BIG = -1e30
VMEM_LIMIT = 48 * 1024 * 1024

MM_TM, MM_TN = 1024, 512
CONV_TS = 512
MEM_TQ = 512
LN_TM = 512
ROUTER_TM = 512
MOE_TM = 256
COMB_TC = 128
DSA_KC = 512
PREP_TM = 256


def _cparams(*sem):
    return pltpu.CompilerParams(dimension_semantics=sem, vmem_limit_bytes=VMEM_LIMIT)


def _dot(a, b):
    return jnp.dot(a, b, preferred_element_type=F32)


def _dot_nt(a, b):
    return lax.dot_general(a, b, (((1,), (1,)), ((), ())), preferred_element_type=F32)


def _layer_norm(h, g, b):
    mu = jnp.mean(h, axis=-1, keepdims=True)
    d = h - mu
    var = jnp.mean(d * d, axis=-1, keepdims=True)
    return d * lax.rsqrt(var + LN_EPS) * g + b


def _rope(x, c, s1, s2):
    outs = []
    for k in range(x.shape[1] // LANES):
        xs = x[:, k * LANES:(k + 1) * LANES]
        outs.append(xs * c + pltpu.roll(xs, 8, 1) * s1 + pltpu.roll(xs, LANES - 8, 1) * s2)
    return outs[0] if len(outs) == 1 else jnp.concatenate(outs, axis=1)


def _matmul_kernel(x_ref, w_ref, o_ref):
    o_ref[...] = _dot(x_ref[...].astype(BF16), w_ref[...].astype(BF16)).astype(o_ref.dtype)


def _matmul(x, w, layer, out_dtype, tm, tn):
    m, k = x.shape
    n = w.shape[-1]
    return pl.pallas_call(
        _matmul_kernel,
        grid=(m // tm, n // tn),
        in_specs=[pl.BlockSpec((tm, k), lambda i, j: (i, 0)),
                  pl.BlockSpec((None, k, tn), lambda i, j: (layer, 0, j))],
        out_specs=pl.BlockSpec((tm, tn), lambda i, j: (i, j)),
        out_shape=jax.ShapeDtypeStruct((m, n), out_dtype),
        compiler_params=_cparams("parallel", "parallel"),
        name="proj_matmul",
    )(x, w)


def _conv_kernel(u_ref, bg_ref, cg_ref, up_ref, cp_ref, w_ref, o_ref, zz_ref, *, tiles_per_seq):
    ts = u_ref.shape[0]
    first = (pl.program_id(0) % tiles_per_seq) == 0
    z = cg_ref[...] * u_ref[...]
    zprev = jnp.where(first, 0.0, cp_ref[...] * up_ref[...])
    zz_ref[0:8, :] = zprev
    zz_ref[8:, :] = z
    w = w_ref[...]
    conv = w[0:1, :] * zz_ref[6:6 + ts, :] + w[1:2, :] * zz_ref[7:7 + ts, :] + w[2:3, :] * z
    o_ref[...] = (bg_ref[...] * conv).astype(o_ref.dtype)


def _conv_mixer(proj, conv_w, layer, seq):
    t = proj.shape[0]
    ts = CONV_TS
    wb = MIX_WIDTH
    prev = lambda col: (lambda i: (jnp.maximum(i * (ts // 8) - 1, 0), col))
    return pl.pallas_call(
        functools.partial(_conv_kernel, tiles_per_seq=seq // ts),
        grid=(t // ts,),
        in_specs=[pl.BlockSpec((ts, wb), lambda i: (i, 0)),
                  pl.BlockSpec((ts, wb), lambda i: (i, 1)),
                  pl.BlockSpec((ts, wb), lambda i: (i, 2)),
                  pl.BlockSpec((8, wb), prev(0)),
                  pl.BlockSpec((8, wb), prev(2)),
                  pl.BlockSpec((None, CONV_WIDTH, wb), lambda i: (layer, 0, 0))],
        out_specs=pl.BlockSpec((ts, wb), lambda i: (i, 0)),
        out_shape=jax.ShapeDtypeStruct((t, wb), BF16),
        scratch_shapes=[pltpu.VMEM((ts + 8, wb), F32)],
        compiler_params=_cparams("parallel"),
        name="conv_mixer",
    )(proj, proj, proj, proj, proj, conv_w)


def _swa_kernel(q_ref, kc_ref, kp_ref, vc_ref, vp_ref, tc_ref, tp_ref, sink_ref, o_ref, *, blocks_per_seq):
    first = (pl.program_id(0) % blocks_per_seq) == 0
    tc = tc_ref[...]
    tp = tp_ref[...]
    q = (_rope(q_ref[...], tc[0], tc[1], tc[2]) * QK_SCALE).astype(BF16)
    k = jnp.concatenate([_rope(kp_ref[...], tp[0], tp[1], tp[2]),
                         _rope(kc_ref[...], tc[0], tc[1], tc[2])], axis=0).astype(BF16)
    v = jnp.concatenate([vp_ref[...], vc_ref[...]], axis=0).astype(BF16)
    qi = lax.broadcasted_iota(I32, (BLOCK, 2 * BLOCK), 0)
    r = lax.broadcasted_iota(I32, (BLOCK, 2 * BLOCK), 1)
    keep = jnp.where(r > qi, jnp.where(r <= qi + BLOCK, 1, 0), 0)
    keep = jnp.where(first, jnp.where(r >= BLOCK, keep, 0), keep) > 0
    sinks = sink_ref[...]
    for h in range(MIX_HEADS):
        g = h // HEADS_PER_KV
        qh = q[:, h * HEAD_DIM:(h + 1) * HEAD_DIM]
        kg = k[:, g * HEAD_DIM:(g + 1) * HEAD_DIM]
        vg = v[:, g * HEAD_DIM:(g + 1) * HEAD_DIM]
        logits = jnp.where(keep, _dot_nt(qh, kg), -jnp.inf)
        sink = sinks[0:1, h:h + 1]
        m = jnp.maximum(jnp.max(logits, axis=-1, keepdims=True), sink)
        e = jnp.exp(logits - m)
        p = e / (jnp.sum(e, axis=-1, keepdims=True) + jnp.exp(sink - m))
        o_ref[:, h * HEAD_DIM:(h + 1) * HEAD_DIM] = _dot(p.astype(BF16), vg).astype(o_ref.dtype)


def _swa_mixer(proj, sinks, layer, rope_tab, seq):
    t = proj.shape[0]
    nbs = seq // BLOCK
    kcol = MIX_WIDTH // KV_WIDTH
    prev_blk = lambda i: jnp.maximum(i - 1, 0)
    return pl.pallas_call(
        functools.partial(_swa_kernel, blocks_per_seq=nbs),
        grid=(t // BLOCK,),
        in_specs=[pl.BlockSpec((BLOCK, MIX_WIDTH), lambda i: (i, 0)),
                  pl.BlockSpec((BLOCK, KV_WIDTH), lambda i: (i, kcol)),
                  pl.BlockSpec((BLOCK, KV_WIDTH), lambda i: (prev_blk(i), kcol)),
                  pl.BlockSpec((BLOCK, KV_WIDTH), lambda i: (i, kcol + 1)),
                  pl.BlockSpec((BLOCK, KV_WIDTH), lambda i: (prev_blk(i), kcol + 1)),
                  pl.BlockSpec((3, BLOCK, LANES), lambda i: (0, i % nbs, 0)),
                  pl.BlockSpec((3, BLOCK, LANES), lambda i: (0, prev_blk(i % nbs), 0)),
                  pl.BlockSpec((None, 1, MIX_HEADS), lambda i: (layer, 0, 0))],
        out_specs=pl.BlockSpec((BLOCK, MIX_WIDTH), lambda i: (i, 0)),
        out_shape=jax.ShapeDtypeStruct((t, MIX_WIDTH), BF16),
        compiler_params=_cparams("parallel"),
        name="swa_mixer",
    )(proj, proj, proj, proj, proj, rope_tab, rope_tab, sinks.reshape(sinks.shape[0], 1, MIX_HEADS))


def _dsa_prep_kernel(cq_ref, k_ref, v_ref, ki_ref, g_ref, wq_ref, wqi_ref, tab_ref,
                     q_out, qi_out, k_out, v_out, ki_out):
    tab = tab_ref[...]
    c, s1, s2 = tab[0], tab[1], tab[2]
    cq = cq_ref[...]
    cqn = cq * lax.rsqrt(jnp.mean(cq * cq, axis=-1, keepdims=True) + LN_EPS) * g_ref[...]
    cqn = cqn.astype(BF16)
    q = _dot(cqn, wq_ref[...].astype(BF16))
    qi = _dot(cqn, wqi_ref[...].astype(BF16))
    q_out[...] = (_rope(q, c, s1, s2) * QK_SCALE).astype(BF16)
    qi_out[...] = (_rope(qi, c, s1, s2) * (IDX_DIM ** -0.5)).astype(BF16)
    k_out[...] = _rope(k_ref[...], c, s1, s2).astype(BF16)
    v_out[...] = v_ref[...].astype(BF16)
    ki_out[...] = _rope(ki_ref[...], c, s1, s2).astype(BF16)


def _dsa_prep(proj, q_norm_g, w_uq, w_uq_idx, layer, rope_tab, seq):
    t = proj.shape[0]
    tm = PREP_TM
    nts = seq // tm
    qi_w = IDX_HEADS * IDX_DIM
    row = lambda w, col: pl.BlockSpec((tm, w), lambda i: (i, col))
    return pl.pallas_call(
        _dsa_prep_kernel,
        grid=(t // tm,),
        in_specs=[row(Q_RANK, 0), row(KV_WIDTH, 1), row(KV_WIDTH, 2), row(LANES, 8),
                  pl.BlockSpec((None, 1, Q_RANK), lambda i: (layer, 0, 0)),
                  pl.BlockSpec((None, Q_RANK, MIX_WIDTH), lambda i: (layer, 0, 0)),
                  pl.BlockSpec((None, Q_RANK, qi_w), lambda i: (layer, 0, 0)),
                  pl.BlockSpec((3, tm, LANES), lambda i: (0, i % nts, 0))],
        out_specs=[row(MIX_WIDTH, 0), row(qi_w, 0), row(KV_WIDTH, 0), row(KV_WIDTH, 0), row(LANES, 0)],
        out_shape=[jax.ShapeDtypeStruct((t, MIX_WIDTH), BF16), jax.ShapeDtypeStruct((t, qi_w), BF16),
                   jax.ShapeDtypeStruct((t, KV_WIDTH), BF16), jax.ShapeDtypeStruct((t, KV_WIDTH), BF16),
                   jax.ShapeDtypeStruct((t, LANES), BF16)],
        compiler_params=_cparams("parallel"),
        name="dsa_prep",
    )(proj, proj, proj, proj, q_norm_g.reshape(q_norm_g.shape[0], 1, Q_RANK), w_uq, w_uq_idx, rope_tab)


def _dsa_kernel(q_ref, qi_ref, wi_ref, k_ref, v_ref, ki_ref, o_ref, key_ref, bias_ref, *, blocks_per_seq, n_sel):
    seq = k_ref.shape[0]
    kc = min(DSA_KC, seq)
    jb = pl.program_id(0) % blocks_per_seq
    n_chunks = (jb * BLOCK + BLOCK + kc - 1) // kc
    qpos = jb * BLOCK + lax.broadcasted_iota(I32, (BLOCK, 1), 0)

    def chunk_start(c):
        return pl.multiple_of(c * kc, kc)

    def key_positions(c):
        return c * kc + lax.broadcasted_iota(I32, (1, kc), 1)

    wi = wi_ref[...][:, IDX_DIM:IDX_DIM + IDX_HEADS] * (IDX_HEADS ** -0.5)
    qi = qi_ref[...]
    qi_heads = [qi[:, h * IDX_DIM:(h + 1) * IDX_DIM] for h in range(IDX_HEADS)]
    wi_heads = [wi[:, h:h + 1] for h in range(IDX_HEADS)]

    def score_chunk(c, carry):
        off = chunk_start(c)
        kch = ki_ref[pl.ds(off, kc), 0:IDX_DIM]
        score = jnp.zeros((BLOCK, kc), F32)
        for h in range(IDX_HEADS):
            score = score + wi_heads[h] * jnp.maximum(_dot_nt(qi_heads[h], kch), 0.0)
        score = jnp.where(score == 0.0, 0.0, score)
        bits = lax.bitcast_convert_type(score, I32)
        key = bits ^ (lax.shift_right_arithmetic(bits, 31) & 0x7FFFFFFF)
        key_ref[:, pl.ds(off, kc)] = jnp.where(key_positions(c) <= qpos, key, INT_MIN)
        return carry

    lax.fori_loop(0, n_chunks, score_chunk, 0)

    def count(pred):
        def body(c, acc):
            keys = key_ref[:, pl.ds(chunk_start(c), kc)]
            return acc + jnp.sum(jnp.where(pred(keys, c), 1.0, 0.0), axis=1, keepdims=True)
        return lax.fori_loop(0, n_chunks, body, jnp.zeros((BLOCK, 1), F32))

    def value_bit(it, lo):
        cand = lo + lax.shift_left(jnp.int32(1), 31 - it)
        cnt = count(lambda keys, c: keys >= cand)
        return jnp.where(cnt >= n_sel, cand, lo)

    thr = lax.fori_loop(0, 32, value_bit, jnp.full((BLOCK, 1), INT_MIN, I32))
    need = n_sel - count(lambda keys, c: keys > thr)
    n_eq = count(lambda keys, c: keys == thr)
    surplus = jnp.where(n_eq > need, jnp.where(thr > INT_MIN, 1, 0), 0)

    def tie_cut():
        idx_bits = max(seq - 1, 1).bit_length()

        def index_bit(it, cut):
            cand = cut + lax.shift_left(jnp.int32(1), idx_bits - 1 - it)
            cnt = count(lambda keys, c: jnp.logical_and(keys == thr, key_positions(c) < cand))
            return jnp.where(cnt < need, cand, cut)

        cut = lax.fori_loop(0, idx_bits, index_bit, jnp.zeros((BLOCK, 1), I32))
        return jnp.where(surplus > 0, cut, seq)

    cut = lax.cond(jnp.max(surplus) > 0, tie_cut, lambda: jnp.full((BLOCK, 1), seq, I32))

    def bias_chunk(c, carry):
        off = chunk_start(c)
        keys = key_ref[:, pl.ds(off, kc)]
        kpos = key_positions(c)
        tie = jnp.logical_and(keys == thr, kpos <= cut)
        sel = jnp.logical_and(kpos <= qpos, jnp.logical_or(keys > thr, tie))
        bias_ref[:, pl.ds(off, kc)] = jnp.where(sel, 0.0, NEG_BIG)
        return carry

    lax.fori_loop(0, n_chunks, bias_chunk, 0)

    q = q_ref[...]
    for h in range(MIX_HEADS):
        g = h // HEADS_PER_KV
        qh = q[:, h * HEAD_DIM:(h + 1) * HEAD_DIM]

        def attend(c, carry, qh=qh, g=g):
            m, l, acc = carry
            off = chunk_start(c)
            kch = k_ref[pl.ds(off, kc), g * HEAD_DIM:(g + 1) * HEAD_DIM]
            vch = v_ref[pl.ds(off, kc), g * HEAD_DIM:(g + 1) * HEAD_DIM]
            s = _dot_nt(qh, kch) + bias_ref[:, pl.ds(off, kc)]
            m_new = jnp.maximum(m, jnp.max(s, axis=-1, keepdims=True))
            a = jnp.exp(m - m_new)
            p = jnp.exp(s - m_new)
            return (m_new, a * l + jnp.sum(p, axis=-1, keepdims=True), a * acc + _dot(p.astype(BF16), vch))

        init = (jnp.full((BLOCK, 1), NEG_BIG, F32), jnp.zeros((BLOCK, 1), F32), jnp.zeros((BLOCK, HEAD_DIM), F32))
        _, l, acc = lax.fori_loop(0, n_chunks, attend, init)
        o_ref[:, h * HEAD_DIM:(h + 1) * HEAD_DIM] = (acc / l).astype(o_ref.dtype)


def _dsa_attention(q, qi, proj, k, v, ki, seq):
    t = q.shape[0]
    nbs = seq // BLOCK
    n_sel = min(TOPK_MAX, seq // 4)
    per_seq = lambda w: pl.BlockSpec((seq, w), lambda i: (i // nbs, 0))
    return pl.pallas_call(
        functools.partial(_dsa_kernel, blocks_per_seq=nbs, n_sel=n_sel),
        grid=(t // BLOCK,),
        in_specs=[pl.BlockSpec((BLOCK, MIX_WIDTH), lambda i: (i, 0)),
                  pl.BlockSpec((BLOCK, IDX_HEADS * IDX_DIM), lambda i: (i, 0)),
                  pl.BlockSpec((BLOCK, LANES), lambda i: (i, 8)),
                  per_seq(KV_WIDTH), per_seq(KV_WIDTH), per_seq(LANES)],
        out_specs=pl.BlockSpec((BLOCK, MIX_WIDTH), lambda i: (i, 0)),
        out_shape=jax.ShapeDtypeStruct((t, MIX_WIDTH), BF16),
        scratch_shapes=[pltpu.VMEM((BLOCK, seq), I32), pltpu.VMEM((BLOCK, seq), F32)],
        compiler_params=_cparams("parallel"),
        name="dsa_attention",
    )(q, qi, proj, k, v, ki)


def _mem_attn_kernel(q_ref, k_ref, v_ref, o_ref):
    q = (q_ref[...] * QK_SCALE).astype(BF16)
    for h in range(MEM_HEADS):
        sl = slice(h * HEAD_DIM, (h + 1) * HEAD_DIM)
        logits = _dot_nt(q[:, sl], k_ref[:, sl])
        e = jnp.exp(logits - jnp.max(logits, axis=-1, keepdims=True))
        p = e / jnp.sum(e, axis=-1, keepdims=True)
        o_ref[:, sl] = _dot(p.astype(BF16), v_ref[:, sl]).astype(o_ref.dtype)


def _mem_attention(proj, qm_col, kv_mem, seq, mem_len):
    t = proj.shape[0]
    tq = MEM_TQ
    nqs = seq // tq
    return pl.pallas_call(
        _mem_attn_kernel,
        grid=(t // tq,),
        in_specs=[pl.BlockSpec((tq, MEM_WIDTH), lambda i: (i, qm_col)),
                  pl.BlockSpec((mem_len, MEM_WIDTH), lambda i: (i // nqs, 0)),
                  pl.BlockSpec((mem_len, MEM_WIDTH), lambda i: (i // nqs, 1))],
        out_specs=pl.BlockSpec((tq, MEM_WIDTH), lambda i: (i, 0)),
        out_shape=jax.ShapeDtypeStruct((t, MEM_WIDTH), BF16),
        compiler_params=_cparams("parallel"),
        name="mem_attention",
    )(proj, kv_mem, kv_mem)


def _outproj_ln_kernel(x_ref, mix_ref, mo_ref, w_ref, g_ref, b_ref, o_ref, wbf_ref):
    @pl.when(pl.program_id(0) == 0)
    def _():
        wbf_ref[...] = w_ref[...].astype(BF16)

    y = _dot(mix_ref[...], wbf_ref[0:MIX_WIDTH, :]) + _dot(mo_ref[...], wbf_ref[MIX_WIDTH:, :])
    o_ref[...] = _layer_norm(ALPHA * x_ref[...] + y, g_ref[...], b_ref[...])


def _outproj_ln(x, mix, mem_out, w_out, ln_g, ln_b, layer):
    t = x.shape[0]
    tm = LN_TM
    vec = lambda: pl.BlockSpec((None, 1, D_MODEL), lambda i: (layer, 0, 0))
    return pl.pallas_call(
        _outproj_ln_kernel,
        grid=(t // tm,),
        in_specs=[pl.BlockSpec((tm, D_MODEL), lambda i: (i, 0)),
                  pl.BlockSpec((tm, MIX_WIDTH), lambda i: (i, 0)),
                  pl.BlockSpec((tm, MEM_WIDTH), lambda i: (i, 0)),
                  pl.BlockSpec((None, D_MODEL, D_MODEL), lambda i: (layer, 0, 0)),
                  vec(), vec()],
        out_specs=pl.BlockSpec((tm, D_MODEL), lambda i: (i, 0)),
        out_shape=jax.ShapeDtypeStruct((t, D_MODEL), F32),
        scratch_shapes=[pltpu.VMEM((D_MODEL, D_MODEL), BF16)],
        compiler_params=_cparams("arbitrary"),
        name="outproj_ln",
    )(x, mix, mem_out, w_out, ln_g.reshape(DEPTH, 1, D_MODEL), ln_b.reshape(DEPTH, 1, D_MODEL))


def _router_kernel(x_ref, w_ref, b_ref, idx_ref, gate_ref):
    logits = jnp.dot(x_ref[...], w_ref[...], preferred_element_type=F32,
                     precision=lax.Precision.HIGHEST) + b_ref[...]
    lane = lax.broadcasted_iota(I32, logits.shape, 1).astype(F32)
    vals, idxs = [], []
    for _ in range(TOP_K):
        m = jnp.max(logits, axis=-1, keepdims=True)
        ix = jnp.min(jnp.where(logits == m, lane, float(N_EXPERTS)), axis=-1, keepdims=True)
        vals.append(m)
        idxs.append(ix)
        logits = jnp.where(lane == ix, -jnp.inf, logits)
    e = jnp.exp(jnp.concatenate(vals, axis=1) - vals[0])
    idx_ref[...] = jnp.concatenate(idxs, axis=1).astype(I32)
    gate_ref[...] = e / jnp.sum(e, axis=-1, keepdims=True)


def _router(x, router_w, router_b, layer):
    t = x.shape[0]
    tm = ROUTER_TM
    return pl.pallas_call(
        _router_kernel,
        grid=(t // tm,),
        in_specs=[pl.BlockSpec((tm, D_MODEL), lambda i: (i, 0)),
                  pl.BlockSpec((None, D_MODEL, N_EXPERTS), lambda i: (layer, 0, 0)),
                  pl.BlockSpec((None, 1, N_EXPERTS), lambda i: (layer, 0, 0))],
        out_specs=[pl.BlockSpec((tm, TOP_K), lambda i: (i, 0)), pl.BlockSpec((tm, TOP_K), lambda i: (i, 0))],
        out_shape=[jax.ShapeDtypeStruct((t, TOP_K), I32), jax.ShapeDtypeStruct((t, TOP_K), F32)],
        compiler_params=_cparams("parallel"),
        name="router",
    )(x, router_w, router_b.reshape(DEPTH, 1, N_EXPERTS))


def _dispatch_plan(top_idx):
    t = top_idx.shape[0]
    n_assign = t * TOP_K
    tm = MOE_TM
    n_tiles = n_assign // tm + N_EXPERTS
    e = top_idx.reshape(n_assign)
    onehot = (e[:, None] == jnp.arange(N_EXPERTS, dtype=I32)[None, :]).astype(I32)
    csum = jnp.cumsum(onehot, axis=0)
    counts = csum[-1]
    padded = ((counts + tm - 1) // tm) * tm
    group_end = jnp.cumsum(padded)
    group_start = group_end - padded
    slot = jnp.sum((csum - onehot + group_start[None, :]) * onehot, axis=1)
    token = jnp.arange(n_assign, dtype=I32) // TOP_K
    row_ids = jnp.zeros((n_tiles * tm,), I32).at[slot].set(token)
    n_used = group_end[-1] // tm
    tile = jnp.arange(n_tiles, dtype=I32)
    tile_expert = jnp.minimum(jnp.sum((tile[:, None] * tm >= group_end[None, :]).astype(I32), axis=1), N_EXPERTS - 1)
    last_expert = jnp.sum(jnp.where(tile == n_used - 1, tile_expert, 0))
    tile_expert = jnp.where(tile < n_used, tile_expert, last_expert).astype(I32)
    return row_ids.reshape(n_tiles, 1, tm), tile_expert, n_used.reshape(1).astype(I32), slot.reshape(t, TOP_K)


def _gather_rows(src_hbm, ids_ref, n_rows, dst_ref, sem):
    def body(r, carry):
        pltpu.make_async_copy(src_hbm.at[pl.ds(ids_ref[0, 0, r], 1)], dst_ref.at[pl.ds(r, 1)], sem).start()
        return carry
    lax.fori_loop(0, n_rows, body, 0, unroll=8)


def _wait_rows(dst_ref, sem):
    pltpu.make_async_copy(dst_ref, dst_ref, sem).wait()


def _moe_kernel(te_ref, nu_ref, ids_ref, ids_next_ref, x_hbm, wgu_ref, bgu_ref, wd_ref, bd_ref, o_ref,
                xbuf, sems, wgu_bf, wd_bf):
    j = pl.program_id(0)
    n_used = nu_ref[0]
    tm = xbuf.shape[1]

    @pl.when(j == 0)
    def _():
        _gather_rows(x_hbm, ids_ref, tm, xbuf.at[0], sems.at[0])

    @pl.when(j + 1 < n_used)
    def _():
        nxt = (j + 1) % 2
        _gather_rows(x_hbm, ids_next_ref, tm, xbuf.at[nxt], sems.at[nxt])

    @pl.when(j < n_used)
    def _():
        cur = j % 2
        _wait_rows(xbuf.at[cur], sems.at[cur])

        @pl.when(jnp.logical_or(j == 0, te_ref[j] != te_ref[jnp.maximum(j - 1, 0)]))
        def _():
            wgu_bf[...] = wgu_ref[...].astype(BF16)
            wd_bf[...] = wd_ref[...].astype(BF16)

        h = _dot(xbuf[cur].astype(BF16), wgu_bf[...]) + bgu_ref[...]
        gate = jnp.minimum(h[:, :D_EXPERT], SWIGLU_LIMIT)
        up = jnp.clip(h[:, D_EXPERT:], -SWIGLU_LIMIT, SWIGLU_LIMIT)
        act = (up + 1.0) * gate * jax.nn.sigmoid(SWIGLU_ALPHA * gate)
        o_ref[...] = _dot(act.astype(BF16), wd_bf[...]) + bd_ref[...]

    @pl.when(j >= n_used)
    def _():
        o_ref[...] = jnp.zeros_like(o_ref)


def _moe_experts(x, row_ids, tile_expert, n_used, w_gate_up, b_gate_up, w_down, b_down, layer):
    n_tiles = row_ids.shape[0]
    tm = MOE_TM
    grid_spec = pltpu.PrefetchScalarGridSpec(
        num_scalar_prefetch=2,
        grid=(n_tiles,),
        in_specs=[pl.BlockSpec((1, 1, tm), lambda j, te, nu: (j, 0, 0), memory_space=pltpu.SMEM),
                  pl.BlockSpec((1, 1, tm), lambda j, te, nu: (jnp.minimum(j + 1, n_tiles - 1), 0, 0),
                               memory_space=pltpu.SMEM),
                  pl.BlockSpec(memory_space=pl.ANY),
                  pl.BlockSpec((None, None, D_MODEL, 2 * D_EXPERT), lambda j, te, nu: (layer, te[j], 0, 0)),
                  pl.BlockSpec((None, None, 1, 2 * D_EXPERT), lambda j, te, nu: (layer, te[j], 0, 0)),
                  pl.BlockSpec((None, None, D_EXPERT, D_MODEL), lambda j, te, nu: (layer, te[j], 0, 0)),
                  pl.BlockSpec((None, None, 1, D_MODEL), lambda j, te, nu: (layer, te[j], 0, 0))],
        out_specs=pl.BlockSpec((tm, D_MODEL), lambda j, te, nu: (j, 0)),
        scratch_shapes=[pltpu.VMEM((2, tm, D_MODEL), F32), pltpu.SemaphoreType.DMA((2,)),
                        pltpu.VMEM((D_MODEL, 2 * D_EXPERT), BF16), pltpu.VMEM((D_EXPERT, D_MODEL), BF16)],
    )
    return pl.pallas_call(
        _moe_kernel,
        grid_spec=grid_spec,
        out_shape=jax.ShapeDtypeStruct((n_tiles * tm, D_MODEL), F32),
        compiler_params=_cparams("arbitrary"),
        name="moe_experts",
    )(tile_expert, n_used, row_ids, row_ids, x, w_gate_up,
      b_gate_up.reshape(DEPTH, N_EXPERTS, 1, 2 * D_EXPERT), w_down, b_down.reshape(DEPTH, N_EXPERTS, 1, D_MODEL))


def _combine_ln_kernel(slot_ref, slot_next_ref, ys_hbm, gate_ref, x_ref, g_ref, b_ref, o_ref, buf, sems):
    i = pl.program_id(0)
    n = pl.num_programs(0)
    rows = buf.shape[1]
    tc = rows // TOP_K

    @pl.when(i == 0)
    def _():
        _gather_rows(ys_hbm, slot_ref, rows, buf.at[0], sems.at[0])

    @pl.when(i + 1 < n)
    def _():
        nxt = (i + 1) % 2
        _gather_rows(ys_hbm, slot_next_ref, rows, buf.at[nxt], sems.at[nxt])

    cur = i % 2
    _wait_rows(buf.at[cur], sems.at[cur])
    gates = gate_ref[...]
    f = gates[:, 0:1] * buf[cur, 0:tc, :]
    for k in range(1, TOP_K):
        f = f + gates[:, k:k + 1] * buf[cur, k * tc:(k + 1) * tc, :]
    o_ref[...] = _layer_norm(ALPHA * x_ref[...] + f, g_ref[...], b_ref[...])


def _combine_ln(x, ys, slots, gates, ln_g, ln_b, layer):
    t = x.shape[0]
    tc = COMB_TC
    n = t // tc
    slots_km = slots.reshape(n, tc, TOP_K).transpose(0, 2, 1).reshape(n, 1, TOP_K * tc)
    vec = lambda: pl.BlockSpec((None, 1, D_MODEL), lambda i: (layer, 0, 0))
    return pl.pallas_call(
        _combine_ln_kernel,
        grid=(n,),
        in_specs=[pl.BlockSpec((1, 1, TOP_K * tc), lambda i: (i, 0, 0), memory_space=pltpu.SMEM),
                  pl.BlockSpec((1, 1, TOP_K * tc), lambda i: (jnp.minimum(i + 1, n - 1), 0, 0),
                               memory_space=pltpu.SMEM),
                  pl.BlockSpec(memory_space=pl.ANY),
                  pl.BlockSpec((tc, TOP_K), lambda i: (i, 0)),
                  pl.BlockSpec((tc, D_MODEL), lambda i: (i, 0)),
                  vec(), vec()],
        out_specs=pl.BlockSpec((tc, D_MODEL), lambda i: (i, 0)),
        out_shape=jax.ShapeDtypeStruct((t, D_MODEL), F32),
        scratch_shapes=[pltpu.VMEM((2, TOP_K * tc, D_MODEL), F32), pltpu.SemaphoreType.DMA((2,))],
        compiler_params=_cparams("arbitrary"),
        name="combine_ln",
    )(slots_km, slots_km, ys, gates, x, ln_g.reshape(DEPTH, 1, D_MODEL), ln_b.reshape(DEPTH, 1, D_MODEL))


def _rope_tables(seq):
    inv = ROPE_THETA ** (-jnp.arange(0, ROT_DIM, 2, dtype=F32) / ROT_DIM)
    ang = jnp.arange(seq, dtype=F32)[:, None] * inv[None, :]
    cos, sin = jnp.cos(ang), jnp.sin(ang)
    half = ROT_DIM // 2
    rest = HEAD_DIM - ROT_DIM
    zeros_h = jnp.zeros((seq, half), F32)
    head = lambda a, b, fill: jnp.concatenate([a, b, jnp.full((seq, rest), fill, F32)], axis=1)
    tab = jnp.stack([head(cos, cos, 1.0), head(zeros_h, sin, 0.0), head(-sin, zeros_h, 0.0)])
    return jnp.tile(tab, (1, 1, LANES // HEAD_DIM))


def kernel(x, mem, a_w_in, a_conv_w, b_w_in, b_q_norm_g, b_w_uq, b_w_uq_idx, c_w_in, c_sinks, w_out, w_kv_mem,
           ln1_g, ln1_b, router_w, router_b, w_gate_up, b_gate_up, w_down, b_down, ln2_g, ln2_b):
    batch, seq, _ = x.shape
    mem_len = mem.shape[1]
    t = batch * seq
    xs = x.reshape(t, D_MODEL)
    mem2 = mem.reshape(batch * mem_len, D_MODEL)
    rope_tab = _rope_tables(seq)

    o1 = Q_RANK
    o3 = o1 + 2 * KV_WIDTH
    o5 = o3 + IDX_DIM + IDX_HEADS
    b_w = jnp.concatenate([b_w_in[:, :, :o3], b_w_in[:, :, o5:], b_w_in[:, :, o3:o5],
                           jnp.zeros(b_w_in.shape[:2] + (LANES - IDX_DIM - IDX_HEADS,), b_w_in.dtype)], axis=2)

    for i in range(DEPTH):
        kind, j = i % N_MIXERS, i // N_MIXERS
        if kind == 0:
            proj = _matmul(xs, a_w_in, j, F32, MM_TM, MM_TN)
            mix = _conv_mixer(proj, a_conv_w, j, seq)
            qm_col = 3 * MIX_WIDTH // MEM_WIDTH
        elif kind == 1:
            proj = _matmul(xs, b_w, j, F32, MM_TM, 3 * LANES)
            q, qi, k, v, ki = _dsa_prep(proj, b_q_norm_g, b_w_uq, b_w_uq_idx, j, rope_tab, seq)
            mix = _dsa_attention(q, qi, proj, k, v, ki, seq)
            qm_col = 3
        else:
            proj = _matmul(xs, c_w_in, j, F32, MM_TM, MM_TN)
            mix = _swa_mixer(proj, c_sinks, j, rope_tab, seq)
            qm_col = (MIX_WIDTH + 2 * KV_WIDTH) // MEM_WIDTH
        kv_mem = _matmul(mem2, w_kv_mem, i, BF16, min(MM_TM, batch * mem_len), MM_TN)
        mem_out = _mem_attention(proj, qm_col, kv_mem, seq, mem_len)
        xs = _outproj_ln(xs, mix, mem_out, w_out, ln1_g, ln1_b, i)
        top_idx, gates = _router(xs, router_w, router_b, i)
        row_ids, tile_expert, n_used, slots = _dispatch_plan(top_idx)
        ys = _moe_experts(xs, row_ids, tile_expert, n_used, w_gate_up, b_gate_up, w_down, b_down, i)
        xs = _combine_ln(xs, ys, slots, gates, ln2_g, ln2_b, i)
    return xs.reshape(batch, seq, D_MODEL)
```

```python
import functools

import jax
import jax.numpy as jnp
from jax import lax
from jax.experimental import pallas as pl
from jax.experimental.pallas import tpu as pltpu

F32 = jnp.float32
BF16 = jnp.bfloat16
I32 = jnp.int32

D_MODEL = 1024
DEPTH = 4
N_MIXERS = 3
HEAD_DIM = 64
MIX_HEADS = 12
MIX_WIDTH = MIX_HEADS * HEAD_DIM
MEM_HEADS = 4
MEM_WIDTH = MEM_HEADS * HEAD_DIM
KV_HEADS = 4
KV_WIDTH = KV_HEADS * HEAD_DIM
HEADS_PER_KV = MIX_HEADS // KV_HEADS
ROT_DIM = HEAD_DIM // 4
ROPE_THETA = 500000.0
CONV_WIDTH = 3
Q_RANK = 256
IDX_HEADS = 8
IDX_DIM = 64
TOPK_MAX = 256
BLOCK = 128
N_EXPERTS = 32
TOP_K = 4
D_EXPERT = 1024
SWIGLU_LIMIT = 7.0
SWIGLU_ALPHA = 1.702
ALPHA = (2 * DEPTH) ** 0.25
LN_EPS = 1e-5

LANES = 128
QK_SCALE = HEAD_DIM ** -0.5
INT_MIN = -(2 ** 31)
NEG_BIG = -1e30
VMEM_LIMIT = 48 * 1024 * 1024

MM_TM, MM_TN = 1024, 512
CONV_TS = 512
MEM_TQ = 512
LN_TM = 512
ROUTER_TM = 512
MOE_TM = 256
COMB_TC = 128
DSA_TQ = 256
DSA_KC = 512
PREP_TM = 256


def _cparams(*sem):
    return pltpu.CompilerParams(dimension_semantics=sem, vmem_limit_bytes=VMEM_LIMIT)


def _dot(a, b):
    return jnp.dot(a, b, preferred_element_type=F32)


def _dot_nt(a, b):
    return lax.dot_general(a, b, (((1,), (1,)), ((), ())), preferred_element_type=F32)


def _layer_norm(h, g, b):
    mu = jnp.mean(h, axis=-1, keepdims=True)
    d = h - mu
    var = jnp.mean(d * d, axis=-1, keepdims=True)
    return d * lax.rsqrt(var + LN_EPS) * g + b


def _rope(x, c, s1, s2):
    outs = []
    for k in range(x.shape[1] // LANES):
        xs = x[:, k * LANES:(k + 1) * LANES]
        outs.append(xs * c + pltpu.roll(xs, 8, 1) * s1 + pltpu.roll(xs, LANES - 8, 1) * s2)
    return outs[0] if len(outs) == 1 else jnp.concatenate(outs, axis=1)


def _matmul_kernel(x_ref, w_ref, o_ref):
    o_ref[...] = _dot(x_ref[...].astype(BF16), w_ref[...].astype(BF16)).astype(o_ref.dtype)


def _matmul(x, w, layer, out_dtype, tm, tn):
    m, k = x.shape
    n = w.shape[-1]
    return pl.pallas_call(
        _matmul_kernel,
        grid=(m // tm, n // tn),
        in_specs=[pl.BlockSpec((tm, k), lambda i, j: (i, 0)),
                  pl.BlockSpec((None, k, tn), lambda i, j: (layer, 0, j))],
        out_specs=pl.BlockSpec((tm, tn), lambda i, j: (i, j)),
        out_shape=jax.ShapeDtypeStruct((m, n), out_dtype),
        compiler_params=_cparams("parallel", "parallel"),
        name="proj_matmul",
    )(x, w)


def _conv_kernel(u_ref, bg_ref, cg_ref, up_ref, cp_ref, w_ref, o_ref, zz_ref, *, tiles_per_seq):
    ts = u_ref.shape[0]
    first = (pl.program_id(0) % tiles_per_seq) == 0
    z = cg_ref[...] * u_ref[...]
    zprev = jnp.where(first, 0.0, cp_ref[...] * up_ref[...])
    zz_ref[0:8, :] = zprev
    zz_ref[8:, :] = z
    w = w_ref[...]
    conv = w[0:1, :] * zz_ref[6:6 + ts, :] + w[1:2, :] * zz_ref[7:7 + ts, :] + w[2:3, :] * z
    o_ref[...] = (bg_ref[...] * conv).astype(o_ref.dtype)


def _conv_mixer(proj, conv_w, layer, seq):
    t = proj.shape[0]
    ts = CONV_TS
    wb = MIX_WIDTH
    prev = lambda col: (lambda i: (jnp.maximum(i * (ts // 8) - 1, 0), col))
    return pl.pallas_call(
        functools.partial(_conv_kernel, tiles_per_seq=seq // ts),
        grid=(t // ts,),
        in_specs=[pl.BlockSpec((ts, wb), lambda i: (i, 0)),
                  pl.BlockSpec((ts, wb), lambda i: (i, 1)),
                  pl.BlockSpec((ts, wb), lambda i: (i, 2)),
                  pl.BlockSpec((8, wb), prev(0)),
                  pl.BlockSpec((8, wb), prev(2)),
                  pl.BlockSpec((None, CONV_WIDTH, wb), lambda i: (layer, 0, 0))],
        out_specs=pl.BlockSpec((ts, wb), lambda i: (i, 0)),
        out_shape=jax.ShapeDtypeStruct((t, wb), BF16),
        scratch_shapes=[pltpu.VMEM((ts + 8, wb), F32)],
        compiler_params=_cparams("parallel"),
        name="conv_mixer",
    )(proj, proj, proj, proj, proj, conv_w)


def _swa_kernel(q_ref, kc_ref, kp_ref, vc_ref, vp_ref, tc_ref, tp_ref, sink_ref, o_ref, *, blocks_per_seq):
    first = (pl.program_id(0) % blocks_per_seq) == 0
    tc = tc_ref[...]
    tp = tp_ref[...]
    q = (_rope(q_ref[...], tc[0], tc[1], tc[2]) * QK_SCALE).astype(BF16)
    k = jnp.concatenate([_rope(kp_ref[...], tp[0], tp[1], tp[2]),
                         _rope(kc_ref[...], tc[0], tc[1], tc[2])], axis=0).astype(BF16)
    v = jnp.concatenate([vp_ref[...], vc_ref[...]], axis=0).astype(BF16)
    qi = lax.broadcasted_iota(I32, (BLOCK, 2 * BLOCK), 0)
    r = lax.broadcasted_iota(I32, (BLOCK, 2 * BLOCK), 1)
    keep = jnp.where(r > qi, jnp.where(r <= qi + BLOCK, 1, 0), 0)
    keep = jnp.where(first, jnp.where(r >= BLOCK, keep, 0), keep) > 0
    sinks = sink_ref[...]
    for h in range(MIX_HEADS):
        g = h // HEADS_PER_KV
        qh = q[:, h * HEAD_DIM:(h + 1) * HEAD_DIM]
        kg = k[:, g * HEAD_DIM:(g + 1) * HEAD_DIM]
        vg = v[:, g * HEAD_DIM:(g + 1) * HEAD_DIM]
        logits = jnp.where(keep, _dot_nt(qh, kg), -jnp.inf)
        sink = sinks[0:1, h:h + 1]
        m = jnp.maximum(jnp.max(logits, axis=-1, keepdims=True), sink)
        e = jnp.exp(logits - m)
        p = e / (jnp.sum(e, axis=-1, keepdims=True) + jnp.exp(sink - m))
        o_ref[:, h * HEAD_DIM:(h + 1) * HEAD_DIM] = _dot(p.astype(BF16), vg).astype(o_ref.dtype)


def _swa_mixer(proj, sinks, layer, rope_tab, seq):
    t = proj.shape[0]
    nbs = seq // BLOCK
    kcol = MIX_WIDTH // KV_WIDTH
    prev_blk = lambda i: jnp.maximum(i - 1, 0)
    return pl.pallas_call(
        functools.partial(_swa_kernel, blocks_per_seq=nbs),
        grid=(t // BLOCK,),
        in_specs=[pl.BlockSpec((BLOCK, MIX_WIDTH), lambda i: (i, 0)),
                  pl.BlockSpec((BLOCK, KV_WIDTH), lambda i: (i, kcol)),
                  pl.BlockSpec((BLOCK, KV_WIDTH), lambda i: (prev_blk(i), kcol)),
                  pl.BlockSpec((BLOCK, KV_WIDTH), lambda i: (i, kcol + 1)),
                  pl.BlockSpec((BLOCK, KV_WIDTH), lambda i: (prev_blk(i), kcol + 1)),
                  pl.BlockSpec((3, BLOCK, LANES), lambda i: (0, i % nbs, 0)),
                  pl.BlockSpec((3, BLOCK, LANES), lambda i: (0, prev_blk(i % nbs), 0)),
                  pl.BlockSpec((None, 1, MIX_HEADS), lambda i: (layer, 0, 0))],
        out_specs=pl.BlockSpec((BLOCK, MIX_WIDTH), lambda i: (i, 0)),
        out_shape=jax.ShapeDtypeStruct((t, MIX_WIDTH), BF16),
        compiler_params=_cparams("parallel"),
        name="swa_mixer",
    )(proj, proj, proj, proj, proj, rope_tab, rope_tab, sinks.reshape(sinks.shape[0], 1, MIX_HEADS))


def _dsa_prep_kernel(cq_ref, k_ref, v_ref, ki_ref, g_ref, wq_ref, wqi_ref, tab_ref,
                     q_out, qi_out, k_out, v_out, ki_out):
    tab = tab_ref[...]
    c, s1, s2 = tab[0], tab[1], tab[2]
    cq = cq_ref[...]
    cqn = cq * lax.rsqrt(jnp.mean(cq * cq, axis=-1, keepdims=True) + LN_EPS) * g_ref[...]
    cqn = cqn.astype(BF16)
    q = _dot(cqn, wq_ref[...].astype(BF16))
    qi = _dot(cqn, wqi_ref[...].astype(BF16))
    q_out[...] = (_rope(q, c, s1, s2) * QK_SCALE).astype(BF16)
    qi_out[...] = (_rope(qi, c, s1, s2) * (IDX_DIM ** -0.5)).astype(BF16)
    k_out[...] = _rope(k_ref[...], c, s1, s2).astype(BF16)
    v_out[...] = v_ref[...].astype(BF16)
    ki_out[...] = _rope(ki_ref[...], c, s1, s2).astype(BF16)


def _dsa_prep(proj, q_norm_g, w_uq, w_uq_idx, layer, rope_tab, seq):
    t = proj.shape[0]
    tm = PREP_TM
    nts = seq // tm
    qi_w = IDX_HEADS * IDX_DIM
    row = lambda w, col: pl.BlockSpec((tm, w), lambda i: (i, col))
    return pl.pallas_call(
        _dsa_prep_kernel,
        grid=(t // tm,),
        in_specs=[row(Q_RANK, 0), row(KV_WIDTH, 1), row(KV_WIDTH, 2), row(LANES, 8),
                  pl.BlockSpec((None, 1, Q_RANK), lambda i: (layer, 0, 0)),
                  pl.BlockSpec((None, Q_RANK, MIX_WIDTH), lambda i: (layer, 0, 0)),
                  pl.BlockSpec((None, Q_RANK, qi_w), lambda i: (layer, 0, 0)),
                  pl.BlockSpec((3, tm, LANES), lambda i: (0, i % nts, 0))],
        out_specs=[row(MIX_WIDTH, 0), row(qi_w, 0), row(KV_WIDTH, 0), row(KV_WIDTH, 0), row(LANES, 0)],
        out_shape=[jax.ShapeDtypeStruct((t, MIX_WIDTH), BF16), jax.ShapeDtypeStruct((t, qi_w), BF16),
                   jax.ShapeDtypeStruct((t, KV_WIDTH), BF16), jax.ShapeDtypeStruct((t, KV_WIDTH), BF16),
                   jax.ShapeDtypeStruct((t, LANES), BF16)],
        compiler_params=_cparams("parallel"),
        name="dsa_prep",
    )(proj, proj, proj, proj, q_norm_g.reshape(q_norm_g.shape[0], 1, Q_RANK), w_uq, w_uq_idx, rope_tab)


def _lane_fold(x, op, acc):
    for t in range(x.shape[1] // LANES):
        acc = op(acc, x[:, t * LANES:(t + 1) * LANES])
    return acc


def _key_to_float(key):
    return lax.bitcast_convert_type(key ^ (lax.shift_right_arithmetic(key, 31) & 0x7FFFFFFF), F32)


KEY_NEG_INF = INT_MIN + 2 ** 23 - 1


def _dsa_kernel(q_ref, qi_ref, wi_ref, k_ref, v_ref, ki_ref, o_ref, score_ref, s_ref, m_ref, l_ref, acc_ref,
                *, tiles_per_seq, n_sel):
    tq = q_ref.shape[0]
    seq = k_ref.shape[0]
    kc = min(DSA_KC, seq)
    jt = pl.program_id(0) % tiles_per_seq
    n_chunks = (jt * tq + tq + kc - 1) // kc
    qpos = jt * tq + lax.broadcasted_iota(I32, (tq, 1), 0)

    def chunk_start(c):
        return pl.multiple_of(c * kc, kc)

    def key_positions(c):
        return c * kc + lax.broadcasted_iota(I32, (1, kc), 1)

    wi = wi_ref[...][:, IDX_DIM:IDX_DIM + IDX_HEADS] * (IDX_HEADS ** -0.5)
    qi = qi_ref[...]
    qi_heads = [qi[:, h * IDX_DIM:(h + 1) * IDX_DIM] for h in range(IDX_HEADS)]
    wi_heads = [wi[:, h:h + 1] for h in range(IDX_HEADS)]

    def score_chunk(c, carry):
        off = chunk_start(c)
        kch = ki_ref[pl.ds(off, kc), 0:IDX_DIM]
        score = jnp.zeros((tq, kc), F32)
        for h in range(IDX_HEADS):
            score = score + wi_heads[h] * jnp.maximum(_dot_nt(qi_heads[h], kch), 0.0)
        score_ref[:, pl.ds(off, kc)] = jnp.where(key_positions(c) <= qpos, score, -jnp.inf)
        return carry

    lax.fori_loop(0, n_chunks, score_chunk, 0)

    def count(pred):
        def body(c, acc):
            sc = score_ref[:, pl.ds(chunk_start(c), kc)]
            return _lane_fold(jnp.where(pred(sc, c), 1.0, 0.0), jnp.add, acc)
        acc = lax.fori_loop(0, n_chunks, body, jnp.zeros((tq, LANES), F32))
        return jnp.sum(acc, axis=1, keepdims=True)

    def value_bit(it, lo):
        cand = lo + lax.shift_left(jnp.int32(1), 31 - it)
        cand_f = _key_to_float(cand)
        cnt = count(lambda sc, c: sc >= cand_f)
        take = jnp.logical_or(cnt >= n_sel, cand < KEY_NEG_INF)
        return jnp.where(take, cand, lo)

    lo = lax.fori_loop(0, 32, value_bit, jnp.full((tq, 1), INT_MIN, I32))
    thr = _key_to_float(jnp.maximum(lo, KEY_NEG_INF))
    need = n_sel - count(lambda sc, c: sc > thr)
    n_eq = count(lambda sc, c: sc == thr)
    surplus = jnp.where(n_eq > need, jnp.where(lo > KEY_NEG_INF, 1, 0), 0)

    def tie_cut():
        idx_bits = max(seq - 1, 1).bit_length()

        def index_bit(it, cut):
            cand = cut + lax.shift_left(jnp.int32(1), idx_bits - 1 - it)
            cnt = count(lambda sc, c: jnp.logical_and(sc == thr, key_positions(c) < cand))
            return jnp.where(cnt < need, cand, cut)

        cut = lax.fori_loop(0, idx_bits, index_bit, jnp.zeros((tq, 1), I32))
        return jnp.where(surplus > 0, cut, seq)

    cut = lax.cond(jnp.max(surplus) > 0, tie_cut, lambda: jnp.full((tq, 1), seq, I32))

    def bias_chunk(c, carry):
        off = chunk_start(c)
        sc = score_ref[:, pl.ds(off, kc)]
        kpos = key_positions(c)
        tie = jnp.logical_and(sc == thr, kpos <= cut)
        sel = jnp.logical_and(kpos <= qpos, jnp.logical_or(sc > thr, tie))
        score_ref[:, pl.ds(off, kc)] = jnp.where(sel, 0.0, NEG_BIG)
        return carry

    lax.fori_loop(0, n_chunks, bias_chunk, 0)

    q = q_ref[...]
    for g in range(KV_HEADS):
        heads = range(g * HEADS_PER_KV, (g + 1) * HEADS_PER_KV)
        qg = jnp.concatenate([q[:, h * HEAD_DIM:(h + 1) * HEAD_DIM] for h in heads], axis=0)
        kv_cols = slice(g * HEAD_DIM, (g + 1) * HEAD_DIM)
        m_ref[...] = jnp.full(m_ref.shape, NEG_BIG, F32)
        l_ref[...] = jnp.zeros(l_ref.shape, F32)
        acc_ref[...] = jnp.zeros(acc_ref.shape, F32)

        def logits_chunk(c, carry, qg=qg, kv_cols=kv_cols):
            off = chunk_start(c)
            bias = score_ref[:, pl.ds(off, kc)]
            s = _dot_nt(qg, k_ref[pl.ds(off, kc), kv_cols]) + jnp.concatenate([bias] * HEADS_PER_KV, axis=0)
            s_ref[:, pl.ds(off, kc)] = s
            m_ref[...] = _lane_fold(s, jnp.maximum, m_ref[...])
            return carry

        lax.fori_loop(0, n_chunks, logits_chunk, 0)
        m = jnp.max(m_ref[...], axis=1, keepdims=True)

        def pv_chunk(c, carry, m=m, kv_cols=kv_cols):
            off = chunk_start(c)
            p = jnp.exp(s_ref[:, pl.ds(off, kc)] - m)
            l_ref[...] = _lane_fold(p, jnp.add, l_ref[...])
            acc_ref[...] += _dot(p.astype(BF16), v_ref[pl.ds(off, kc), kv_cols])
            return carry

        lax.fori_loop(0, n_chunks, pv_chunk, 0)
        out = acc_ref[...] / jnp.sum(l_ref[...], axis=1, keepdims=True)
        for r, h in enumerate(heads):
            o_ref[:, h * HEAD_DIM:(h + 1) * HEAD_DIM] = out[r * tq:(r + 1) * tq].astype(o_ref.dtype)


def _dsa_attention(q, qi, proj, k, v, ki, seq):
    t = q.shape[0]
    tq = DSA_TQ
    nts = seq // tq
    n_sel = min(TOPK_MAX, seq // 4)
    wi_col = (Q_RANK + 2 * KV_WIDTH + MEM_WIDTH) // LANES
    per_seq = lambda w: pl.BlockSpec((seq, w), lambda i: (i // nts, 0))
    return pl.pallas_call(
        functools.partial(_dsa_kernel, tiles_per_seq=nts, n_sel=n_sel),
        grid=(t // tq,),
        in_specs=[pl.BlockSpec((tq, MIX_WIDTH), lambda i: (i, 0)),
                  pl.BlockSpec((tq, IDX_HEADS * IDX_DIM), lambda i: (i, 0)),
                  pl.BlockSpec((tq, LANES), lambda i: (i, wi_col)),
                  per_seq(KV_WIDTH), per_seq(KV_WIDTH), per_seq(LANES)],
        out_specs=pl.BlockSpec((tq, MIX_WIDTH), lambda i: (i, 0)),
        out_shape=jax.ShapeDtypeStruct((t, MIX_WIDTH), BF16),
        scratch_shapes=[pltpu.VMEM((tq, seq), F32),
                        pltpu.VMEM((HEADS_PER_KV * tq, seq), F32),
                        pltpu.VMEM((HEADS_PER_KV * tq, LANES), F32),
                        pltpu.VMEM((HEADS_PER_KV * tq, LANES), F32),
                        pltpu.VMEM((HEADS_PER_KV * tq, HEAD_DIM), F32)],
        compiler_params=_cparams("parallel"),
        name="dsa_attention",
    )(q, qi, proj, k, v, ki)


def _mem_attn_kernel(q_ref, k_ref, v_ref, o_ref):
    q = (q_ref[...] * QK_SCALE).astype(BF16)
    for h in range(MEM_HEADS):
        sl = slice(h * HEAD_DIM, (h + 1) * HEAD_DIM)
        logits = _dot_nt(q[:, sl], k_ref[:, sl])
        e = jnp.exp(logits - jnp.max(logits, axis=-1, keepdims=True))
        p = e / jnp.sum(e, axis=-1, keepdims=True)
        o_ref[:, sl] = _dot(p.astype(BF16), v_ref[:, sl]).astype(o_ref.dtype)


def _mem_attention(proj, qm_col, kv_mem, seq, mem_len):
    t = proj.shape[0]
    tq = MEM_TQ
    nqs = seq // tq
    return pl.pallas_call(
        _mem_attn_kernel,
        grid=(t // tq,),
        in_specs=[pl.BlockSpec((tq, MEM_WIDTH), lambda i: (i, qm_col)),
                  pl.BlockSpec((mem_len, MEM_WIDTH), lambda i: (i // nqs, 0)),
                  pl.BlockSpec((mem_len, MEM_WIDTH), lambda i: (i // nqs, 1))],
        out_specs=pl.BlockSpec((tq, MEM_WIDTH), lambda i: (i, 0)),
        out_shape=jax.ShapeDtypeStruct((t, MEM_WIDTH), BF16),
        compiler_params=_cparams("parallel"),
        name="mem_attention",
    )(proj, kv_mem, kv_mem)


def _outproj_ln_kernel(x_ref, mix_ref, mo_ref, w_ref, g_ref, b_ref, o_ref, wbf_ref):
    @pl.when(pl.program_id(0) == 0)
    def _():
        wbf_ref[...] = w_ref[...].astype(BF16)

    y = _dot(mix_ref[...], wbf_ref[0:MIX_WIDTH, :]) + _dot(mo_ref[...], wbf_ref[MIX_WIDTH:, :])
    o_ref[...] = _layer_norm(ALPHA * x_ref[...] + y, g_ref[...], b_ref[...])


def _outproj_ln(x, mix, mem_out, w_out, ln_g, ln_b, layer):
    t = x.shape[0]
    tm = LN_TM
    vec = lambda: pl.BlockSpec((None, 1, D_MODEL), lambda i: (layer, 0, 0))
    return pl.pallas_call(
        _outproj_ln_kernel,
        grid=(t // tm,),
        in_specs=[pl.BlockSpec((tm, D_MODEL), lambda i: (i, 0)),
                  pl.BlockSpec((tm, MIX_WIDTH), lambda i: (i, 0)),
                  pl.BlockSpec((tm, MEM_WIDTH), lambda i: (i, 0)),
                  pl.BlockSpec((None, D_MODEL, D_MODEL), lambda i: (layer, 0, 0)),
                  vec(), vec()],
        out_specs=pl.BlockSpec((tm, D_MODEL), lambda i: (i, 0)),
        out_shape=jax.ShapeDtypeStruct((t, D_MODEL), F32),
        scratch_shapes=[pltpu.VMEM((D_MODEL, D_MODEL), BF16)],
        compiler_params=_cparams("arbitrary"),
        name="outproj_ln",
    )(x, mix, mem_out, w_out, ln_g.reshape(DEPTH, 1, D_MODEL), ln_b.reshape(DEPTH, 1, D_MODEL))


def _router_kernel(x_ref, w_ref, b_ref, idx_ref, gate_ref):
    logits = jnp.dot(x_ref[...], w_ref[...], preferred_element_type=F32,
                     precision=lax.Precision.HIGHEST) + b_ref[...]
    lane = lax.broadcasted_iota(I32, logits.shape, 1).astype(F32)
    vals, idxs = [], []
    for _ in range(TOP_K):
        m = jnp.max(logits, axis=-1, keepdims=True)
        ix = jnp.min(jnp.where(logits == m, lane, float(N_EXPERTS)), axis=-1, keepdims=True)
        vals.append(m)
        idxs.append(ix)
        logits = jnp.where(lane == ix, -jnp.inf, logits)
    e = jnp.exp(jnp.concatenate(vals, axis=1) - vals[0])
    idx_ref[...] = jnp.concatenate(idxs, axis=1).astype(I32)
    gate_ref[...] = e / jnp.sum(e, axis=-1, keepdims=True)


def _router(x, router_w, router_b, layer):
    t = x.shape[0]
    tm = ROUTER_TM
    return pl.pallas_call(
        _router_kernel,
        grid=(t // tm,),
        in_specs=[pl.BlockSpec((tm, D_MODEL), lambda i: (i, 0)),
                  pl.BlockSpec((None, D_MODEL, N_EXPERTS), lambda i: (layer, 0, 0)),
                  pl.BlockSpec((None, 1, N_EXPERTS), lambda i: (layer, 0, 0))],
        out_specs=[pl.BlockSpec((tm, TOP_K), lambda i: (i, 0)), pl.BlockSpec((tm, TOP_K), lambda i: (i, 0))],
        out_shape=[jax.ShapeDtypeStruct((t, TOP_K), I32), jax.ShapeDtypeStruct((t, TOP_K), F32)],
        compiler_params=_cparams("parallel"),
        name="router",
    )(x, router_w, router_b.reshape(DEPTH, 1, N_EXPERTS))


def _dispatch_plan(top_idx):
    t = top_idx.shape[0]
    n_assign = t * TOP_K
    tm = MOE_TM
    n_tiles = n_assign // tm + N_EXPERTS
    e = top_idx.reshape(n_assign)
    onehot = (e[:, None] == jnp.arange(N_EXPERTS, dtype=I32)[None, :]).astype(I32)
    csum = jnp.cumsum(onehot, axis=0)
    counts = csum[-1]
    padded = ((counts + tm - 1) // tm) * tm
    group_end = jnp.cumsum(padded)
    group_start = group_end - padded
    slot = jnp.sum((csum - onehot + group_start[None, :]) * onehot, axis=1)
    token = jnp.arange(n_assign, dtype=I32) // TOP_K
    row_ids = jnp.zeros((n_tiles * tm,), I32).at[slot].set(token)
    n_used = group_end[-1] // tm
    tile = jnp.arange(n_tiles, dtype=I32)
    tile_expert = jnp.minimum(jnp.sum((tile[:, None] * tm >= group_end[None, :]).astype(I32), axis=1), N_EXPERTS - 1)
    last_expert = jnp.sum(jnp.where(tile == n_used - 1, tile_expert, 0))
    tile_expert = jnp.where(tile < n_used, tile_expert, last_expert).astype(I32)
    return row_ids.reshape(n_tiles, 1, tm), tile_expert, n_used.reshape(1).astype(I32), slot.reshape(t, TOP_K)


def _gather_rows(src_hbm, ids_ref, n_rows, dst_ref, sem):
    def body(r, carry):
        pltpu.make_async_copy(src_hbm.at[pl.ds(ids_ref[0, 0, r], 1)], dst_ref.at[pl.ds(r, 1)], sem).start()
        return carry
    lax.fori_loop(0, n_rows, body, 0, unroll=8)


def _wait_rows(dst_ref, sem):
    pltpu.make_async_copy(dst_ref, dst_ref, sem).wait()


def _moe_kernel(te_ref, nu_ref, ids_ref, ids_next_ref, x_hbm, wgu_ref, bgu_ref, wd_ref, bd_ref, o_ref,
                xbuf0, xbuf1, sems, wgu_bf, wd_bf):
    j = pl.program_id(0)
    n_used = nu_ref[0]
    tm = xbuf0.shape[0]
    bufs = ((xbuf0, sems.at[0]), (xbuf1, sems.at[1]))

    @pl.when(j == 0)
    def _():
        _gather_rows(x_hbm, ids_ref, tm, xbuf0, sems.at[0])

    def expert_tile(cur, nxt):
        _wait_rows(*cur)

        @pl.when(jnp.logical_or(j == 0, te_ref[j] != te_ref[jnp.maximum(j - 1, 0)]))
        def _():
            wgu_bf[...] = wgu_ref[...].astype(BF16)
            wd_bf[...] = wd_ref[...].astype(BF16)

        for r in range(tm):
            pltpu.make_async_copy(x_hbm.at[pl.ds(ids_next_ref[0, 0, r], 1)], nxt[0].at[pl.ds(r, 1)], nxt[1]).start()
        h = _dot(cur[0][...].astype(BF16), wgu_bf[...]) + bgu_ref[...]
        gate = jnp.minimum(h[:, :D_EXPERT], SWIGLU_LIMIT)
        up = jnp.clip(h[:, D_EXPERT:], -SWIGLU_LIMIT, SWIGLU_LIMIT)
        act = (up + 1.0) * gate * jax.nn.sigmoid(SWIGLU_ALPHA * gate)
        o_ref[...] = _dot(act.astype(BF16), wd_bf[...]) + bd_ref[...]

    for parity in range(2):
        @pl.when(jnp.logical_and(j < n_used, j % 2 == parity))
        def _(parity=parity):
            expert_tile(bufs[parity], bufs[1 - parity])

        @pl.when(jnp.logical_and(j == n_used, j % 2 == parity))
        def _(parity=parity):
            _wait_rows(*bufs[parity])

    @pl.when(j >= n_used)
    def _():
        o_ref[...] = jnp.zeros_like(o_ref)


def _moe_experts(x, row_ids, tile_expert, n_used, w_gate_up, b_gate_up, w_down, b_down, layer):
    n_tiles = row_ids.shape[0]
    tm = MOE_TM
    grid_spec = pltpu.PrefetchScalarGridSpec(
        num_scalar_prefetch=2,
        grid=(n_tiles,),
        in_specs=[pl.BlockSpec((1, 1, tm), lambda j, te, nu: (j, 0, 0), memory_space=pltpu.SMEM),
                  pl.BlockSpec((1, 1, tm), lambda j, te, nu: (jnp.minimum(j + 1, n_tiles - 1), 0, 0),
                               memory_space=pltpu.SMEM),
                  pl.BlockSpec(memory_space=pl.ANY),
                  pl.BlockSpec((None, None, D_MODEL, 2 * D_EXPERT), lambda j, te, nu: (layer, te[j], 0, 0)),
                  pl.BlockSpec((None, None, 1, 2 * D_EXPERT), lambda j, te, nu: (layer, te[j], 0, 0)),
                  pl.BlockSpec((None, None, D_EXPERT, D_MODEL), lambda j, te, nu: (layer, te[j], 0, 0)),
                  pl.BlockSpec((None, None, 1, D_MODEL), lambda j, te, nu: (layer, te[j], 0, 0))],
        out_specs=pl.BlockSpec((tm, D_MODEL), lambda j, te, nu: (j, 0)),
        scratch_shapes=[pltpu.VMEM((tm, D_MODEL), F32), pltpu.VMEM((tm, D_MODEL), F32),
                        pltpu.SemaphoreType.DMA((2,)),
                        pltpu.VMEM((D_MODEL, 2 * D_EXPERT), BF16), pltpu.VMEM((D_EXPERT, D_MODEL), BF16)],
    )
    return pl.pallas_call(
        _moe_kernel,
        grid_spec=grid_spec,
        out_shape=jax.ShapeDtypeStruct((n_tiles * tm, D_MODEL), F32),
        compiler_params=_cparams("arbitrary"),
        name="moe_experts",
    )(tile_expert, n_used, row_ids, row_ids, x, w_gate_up,
      b_gate_up.reshape(DEPTH, N_EXPERTS, 1, 2 * D_EXPERT), w_down, b_down.reshape(DEPTH, N_EXPERTS, 1, D_MODEL))


def _combine_ln_kernel(slot_ref, slot_next_ref, ys_hbm, gate_ref, x_ref, g_ref, b_ref, o_ref, buf0, buf1, sems,
                       *, n_steps):
    i = pl.program_id(0)
    rows = buf0.shape[0]
    tc = rows // TOP_K
    bufs = ((buf0, sems.at[0]), (buf1, sems.at[1]))

    @pl.when(i == 0)
    def _():
        _gather_rows(ys_hbm, slot_ref, rows, buf0, sems.at[0])

    def token_tile(cur, nxt):
        _wait_rows(*cur)
        if nxt is not None:
            for r in range(rows):
                pltpu.make_async_copy(ys_hbm.at[pl.ds(slot_next_ref[0, 0, r], 1)], nxt[0].at[pl.ds(r, 1)],
                                      nxt[1]).start(priority=r % 2)
        gates = gate_ref[...]
        f = gates[:, 0:1] * cur[0][0:tc, :]
        for k in range(1, TOP_K):
            f = f + gates[:, k:k + 1] * cur[0][k * tc:(k + 1) * tc, :]
        o_ref[...] = _layer_norm(ALPHA * x_ref[...] + f, g_ref[...], b_ref[...])

    for parity in range(2):
        @pl.when(jnp.logical_and(i < n_steps - 1, i % 2 == parity))
        def _(parity=parity):
            token_tile(bufs[parity], bufs[1 - parity])

    @pl.when(i == n_steps - 1)
    def _():
        token_tile(bufs[(n_steps - 1) % 2], None)


def _combine_ln(x, ys, slots, gates, ln_g, ln_b, layer):
    t = x.shape[0]
    tc = COMB_TC
    n = t // tc
    slots_km = slots.reshape(n, tc, TOP_K).transpose(0, 2, 1).reshape(n, 1, TOP_K * tc)
    vec = lambda: pl.BlockSpec((None, 1, D_MODEL), lambda i: (layer, 0, 0))
    return pl.pallas_call(
        functools.partial(_combine_ln_kernel, n_steps=n),
        grid=(n,),
        in_specs=[pl.BlockSpec((1, 1, TOP_K * tc), lambda i: (i, 0, 0), memory_space=pltpu.SMEM),
                  pl.BlockSpec((1, 1, TOP_K * tc), lambda i: (jnp.minimum(i + 1, n - 1), 0, 0),
                               memory_space=pltpu.SMEM),
                  pl.BlockSpec(memory_space=pl.ANY),
                  pl.BlockSpec((tc, TOP_K), lambda i: (i, 0)),
                  pl.BlockSpec((tc, D_MODEL), lambda i: (i, 0)),
                  vec(), vec()],
        out_specs=pl.BlockSpec((tc, D_MODEL), lambda i: (i, 0)),
        out_shape=jax.ShapeDtypeStruct((t, D_MODEL), F32),
        scratch_shapes=[pltpu.VMEM((TOP_K * tc, D_MODEL), F32), pltpu.VMEM((TOP_K * tc, D_MODEL), F32),
                        pltpu.SemaphoreType.DMA((2,))],
        compiler_params=_cparams("arbitrary"),
        name="combine_ln",
    )(slots_km, slots_km, ys, gates, x, ln_g.reshape(DEPTH, 1, D_MODEL), ln_b.reshape(DEPTH, 1, D_MODEL))


def _rope_tables(seq):
    inv = ROPE_THETA ** (-jnp.arange(0, ROT_DIM, 2, dtype=F32) / ROT_DIM)
    ang = jnp.arange(seq, dtype=F32)[:, None] * inv[None, :]
    cos, sin = jnp.cos(ang), jnp.sin(ang)
    half = ROT_DIM // 2
    rest = HEAD_DIM - ROT_DIM
    zeros_h = jnp.zeros((seq, half), F32)
    head = lambda a, b, fill: jnp.concatenate([a, b, jnp.full((seq, rest), fill, F32)], axis=1)
    tab = jnp.stack([head(cos, cos, 1.0), head(zeros_h, sin, 0.0), head(-sin, zeros_h, 0.0)])
    return jnp.tile(tab, (1, 1, LANES // HEAD_DIM))


def kernel(x, mem, a_w_in, a_conv_w, b_w_in, b_q_norm_g, b_w_uq, b_w_uq_idx, c_w_in, c_sinks, w_out, w_kv_mem,
           ln1_g, ln1_b, router_w, router_b, w_gate_up, b_gate_up, w_down, b_down, ln2_g, ln2_b):
    batch, seq, _ = x.shape
    mem_len = mem.shape[1]
    t = batch * seq
    xs = x.reshape(t, D_MODEL)
    mem2 = mem.reshape(batch * mem_len, D_MODEL)
    rope_tab = _rope_tables(seq)

    o1 = Q_RANK
    o3 = o1 + 2 * KV_WIDTH
    o5 = o3 + IDX_DIM + IDX_HEADS
    b_w = jnp.concatenate([b_w_in[:, :, :o3], b_w_in[:, :, o5:], b_w_in[:, :, o3:o5],
                           jnp.zeros(b_w_in.shape[:2] + (LANES - IDX_DIM - IDX_HEADS,), b_w_in.dtype)], axis=2)

    for i in range(DEPTH):
        kind, j = i % N_MIXERS, i // N_MIXERS
        if kind == 0:
            proj = _matmul(xs, a_w_in, j, F32, MM_TM, MM_TN)
            mix = _conv_mixer(proj, a_conv_w, j, seq)
            qm_col = 3 * MIX_WIDTH // MEM_WIDTH
        elif kind == 1:
            proj = _matmul(xs, b_w, j, F32, MM_TM, 3 * LANES)
            q, qi, k, v, ki = _dsa_prep(proj, b_q_norm_g, b_w_uq, b_w_uq_idx, j, rope_tab, seq)
            mix = _dsa_attention(q, qi, proj, k, v, ki, seq)
            qm_col = 3
        else:
            proj = _matmul(xs, c_w_in, j, F32, MM_TM, MM_TN)
            mix = _swa_mixer(proj, c_sinks, j, rope_tab, seq)
            qm_col = (MIX_WIDTH + 2 * KV_WIDTH) // MEM_WIDTH
        kv_mem = _matmul(mem2, w_kv_mem, i, BF16, min(MM_TM, batch * mem_len), MM_TN)
        mem_out = _mem_attention(proj, qm_col, kv_mem, seq, mem_len)
        xs = _outproj_ln(xs, mix, mem_out, w_out, ln1_g, ln1_b, i)
        top_idx, gates = _router(xs, router_w, router_b, i)
        row_ids, tile_expert, n_used, slots = _dispatch_plan(top_idx)
        ys = _moe_experts(xs, row_ids, tile_expert, n_used, w_gate_up, b_gate_up, w_down, b_down, i)
        xs = _combine_ln(xs, ys, slots, gates, ln2_g, ln2_b, i)
    return xs.reshape(batch, seq, D_MODEL)
```

```python
import functools

import jax
import jax.numpy as jnp
from jax import lax
from jax.experimental import pallas as pl
from jax.experimental.pallas import tpu as pltpu

F32 = jnp.float32
BF16 = jnp.bfloat16
I32 = jnp.int32

D_MODEL = 1024
DEPTH = 4
N_MIXERS = 3
HEAD_DIM = 64
MIX_HEADS = 12
MIX_WIDTH = MIX_HEADS * HEAD_DIM
MEM_HEADS = 4
MEM_WIDTH = MEM_HEADS * HEAD_DIM
KV_HEADS = 4
KV_WIDTH = KV_HEADS * HEAD_DIM
HEADS_PER_KV = MIX_HEADS // KV_HEADS
ROT_DIM = HEAD_DIM // 4
ROPE_THETA = 500000.0
CONV_WIDTH = 3
Q_RANK = 256
IDX_HEADS = 8
IDX_DIM = 64
TOPK_MAX = 256
BLOCK = 128
N_EXPERTS = 32
TOP_K = 4
D_EXPERT = 1024
SWIGLU_LIMIT = 7.0
SWIGLU_ALPHA = 1.702
ALPHA = (2 * DEPTH) ** 0.25
LN_EPS = 1e-5

LANES = 128
QK_SCALE = HEAD_DIM ** -0.5
INT_MIN = -(2 ** 31)
NEG_BIG = -1e30
VMEM_LIMIT = 48 * 1024 * 1024

MM_TM, MM_TN = 1024, 512
PROJ_TM = 512
CONV_TS = 512
MEM_TQ = 512
LN_TM = 512
ROUTER_TM = 512
MOE_TM = 256
COMB_TC = 128
DSA_TQ = 256
DSA_KC = 512
PREP_TM = 256


def _cparams(*sem):
    return pltpu.CompilerParams(dimension_semantics=sem, vmem_limit_bytes=VMEM_LIMIT)


def _dot(a, b):
    return jnp.dot(a, b, preferred_element_type=F32)


def _dot_nt(a, b):
    return lax.dot_general(a, b, (((1,), (1,)), ((), ())), preferred_element_type=F32)


def _layer_norm(h, g, b):
    mu = jnp.mean(h, axis=-1, keepdims=True)
    d = h - mu
    var = jnp.mean(d * d, axis=-1, keepdims=True)
    return d * lax.rsqrt(var + LN_EPS) * g + b


def _rope(x, c, s1, s2):
    outs = []
    for k in range(x.shape[1] // LANES):
        xs = x[:, k * LANES:(k + 1) * LANES]
        outs.append(xs * c + pltpu.roll(xs, 8, 1) * s1 + pltpu.roll(xs, LANES - 8, 1) * s2)
    return outs[0] if len(outs) == 1 else jnp.concatenate(outs, axis=1)


def _matmul_kernel(x_ref, w_ref, o_ref):
    o_ref[...] = _dot(x_ref[...].astype(BF16), w_ref[...].astype(BF16)).astype(o_ref.dtype)


def _matmul(x, w, layer, out_dtype, tm, tn):
    m, k = x.shape
    n = w.shape[-1]
    return pl.pallas_call(
        _matmul_kernel,
        grid=(m // tm, n // tn),
        in_specs=[pl.BlockSpec((tm, k), lambda i, j: (i, 0)),
                  pl.BlockSpec((None, k, tn), lambda i, j: (layer, 0, j))],
        out_specs=pl.BlockSpec((tm, tn), lambda i, j: (i, j)),
        out_shape=jax.ShapeDtypeStruct((m, n), out_dtype),
        compiler_params=_cparams("parallel", "parallel"),
        name="proj_matmul",
    )(x, w)


def _conv_kernel(x_ref, w_ref, cw_ref, mix_ref, qm_ref, zz_ref, *, tiles_per_seq):
    ts = x_ref.shape[0]
    wb = MIX_WIDTH

    @pl.when(pl.program_id(0) % tiles_per_seq == 0)
    def _():
        zz_ref[0:8, :] = jnp.zeros((8, wb), F32)

    proj = _dot(x_ref[...].astype(BF16), w_ref[...])
    z = proj[:, 2 * wb:3 * wb] * proj[:, 0:wb]
    zz_ref[8:, :] = z
    cw = cw_ref[...]
    conv = cw[0:1, :] * zz_ref[6:6 + ts, :] + cw[1:2, :] * zz_ref[7:7 + ts, :] + cw[2:3, :] * z
    mix_ref[...] = (proj[:, wb:2 * wb] * conv).astype(mix_ref.dtype)
    qm_ref[...] = (proj[:, 3 * wb:] * QK_SCALE).astype(qm_ref.dtype)
    zz_ref[0:8, :] = zz_ref[ts:ts + 8, :]


def _conv_mixer(x, w_in_bf, conv_w, layer, seq):
    t = x.shape[0]
    ts = CONV_TS
    n_in = w_in_bf.shape[-1]
    return pl.pallas_call(
        functools.partial(_conv_kernel, tiles_per_seq=seq // ts),
        grid=(t // ts,),
        in_specs=[pl.BlockSpec((ts, D_MODEL), lambda i: (i, 0)),
                  pl.BlockSpec((None, D_MODEL, n_in), lambda i: (layer, 0, 0)),
                  pl.BlockSpec((None, CONV_WIDTH, MIX_WIDTH), lambda i: (layer, 0, 0))],
        out_specs=[pl.BlockSpec((ts, MIX_WIDTH), lambda i: (i, 0)), pl.BlockSpec((ts, MEM_WIDTH), lambda i: (i, 0))],
        out_shape=[jax.ShapeDtypeStruct((t, MIX_WIDTH), BF16), jax.ShapeDtypeStruct((t, MEM_WIDTH), BF16)],
        scratch_shapes=[pltpu.VMEM((ts + 8, MIX_WIDTH), F32)],
        compiler_params=_cparams("arbitrary"),
        name="conv_mixer",
    )(x, w_in_bf, conv_w)


def _swa_proj_kernel(x_ref, w_ref, tab_ref, q_out, k_out, v_out, qm_out):
    tab = tab_ref[...]
    c, s1, s2 = tab[0], tab[1], tab[2]
    proj = _dot(x_ref[...].astype(BF16), w_ref[...])
    o1, o2, o3 = MIX_WIDTH, MIX_WIDTH + KV_WIDTH, MIX_WIDTH + 2 * KV_WIDTH
    q_out[...] = (_rope(proj[:, :o1], c, s1, s2) * QK_SCALE).astype(BF16)
    k_out[...] = _rope(proj[:, o1:o2], c, s1, s2).astype(BF16)
    v_out[...] = proj[:, o2:o3].astype(BF16)
    qm_out[...] = (proj[:, o3:] * QK_SCALE).astype(BF16)


def _swa_proj(x, w_in_bf, layer, rope_tab, seq):
    t = x.shape[0]
    tm = PROJ_TM
    nts = seq // tm
    n_in = w_in_bf.shape[-1]
    row = lambda w: pl.BlockSpec((tm, w), lambda i: (i, 0))
    return pl.pallas_call(
        _swa_proj_kernel,
        grid=(t // tm,),
        in_specs=[row(D_MODEL),
                  pl.BlockSpec((None, D_MODEL, n_in), lambda i: (layer, 0, 0)),
                  pl.BlockSpec((3, tm, LANES), lambda i: (0, i % nts, 0))],
        out_specs=[row(MIX_WIDTH), row(KV_WIDTH), row(KV_WIDTH), row(MEM_WIDTH)],
        out_shape=[jax.ShapeDtypeStruct((t, MIX_WIDTH), BF16), jax.ShapeDtypeStruct((t, KV_WIDTH), BF16),
                   jax.ShapeDtypeStruct((t, KV_WIDTH), BF16), jax.ShapeDtypeStruct((t, MEM_WIDTH), BF16)],
        compiler_params=_cparams("parallel"),
        name="swa_proj",
    )(x, w_in_bf, rope_tab)


def _swa_kernel(q_ref, kc_ref, kp_ref, vc_ref, vp_ref, sink_ref, o_ref, *, blocks_per_seq):
    first = (pl.program_id(0) % blocks_per_seq) == 0
    q = q_ref[...]
    k = jnp.concatenate([kp_ref[...], kc_ref[...]], axis=0)
    v = jnp.concatenate([vp_ref[...], vc_ref[...]], axis=0)
    qi = lax.broadcasted_iota(I32, (BLOCK, 2 * BLOCK), 0)
    r = lax.broadcasted_iota(I32, (BLOCK, 2 * BLOCK), 1)
    keep = jnp.where(r > qi, jnp.where(r <= qi + BLOCK, 1, 0), 0)
    keep = jnp.where(first, jnp.where(r >= BLOCK, keep, 0), keep)
    bias = jnp.where(keep > 0, 0.0, -jnp.inf)
    bias = jnp.concatenate([bias] * HEADS_PER_KV, axis=0)
    sinks = sink_ref[...]
    for g in range(KV_HEADS):
        heads = range(g * HEADS_PER_KV, (g + 1) * HEADS_PER_KV)
        kv_cols = slice(g * HEAD_DIM, (g + 1) * HEAD_DIM)
        qg = jnp.concatenate([q[:, h * HEAD_DIM:(h + 1) * HEAD_DIM] for h in heads], axis=0)
        sink = jnp.concatenate([jnp.broadcast_to(sinks[0:1, h:h + 1], (BLOCK, 1)) for h in heads], axis=0)
        logits = _dot_nt(qg, k[:, kv_cols]) + bias
        m = jnp.maximum(jnp.max(logits, axis=-1, keepdims=True), sink)
        e = jnp.exp(logits - m)
        p = e / (jnp.sum(e, axis=-1, keepdims=True) + jnp.exp(sink - m))
        out = _dot(p.astype(BF16), v[:, kv_cols])
        for n, h in enumerate(heads):
            o_ref[:, h * HEAD_DIM:(h + 1) * HEAD_DIM] = out[n * BLOCK:(n + 1) * BLOCK].astype(o_ref.dtype)


def _swa_mixer(q, k, v, sinks, layer, seq):
    t = q.shape[0]
    nbs = seq // BLOCK
    prev_blk = lambda i: jnp.maximum(i - 1, 0)
    return pl.pallas_call(
        functools.partial(_swa_kernel, blocks_per_seq=nbs),
        grid=(t // BLOCK,),
        in_specs=[pl.BlockSpec((BLOCK, MIX_WIDTH), lambda i: (i, 0)),
                  pl.BlockSpec((BLOCK, KV_WIDTH), lambda i: (i, 0)),
                  pl.BlockSpec((BLOCK, KV_WIDTH), lambda i: (prev_blk(i), 0)),
                  pl.BlockSpec((BLOCK, KV_WIDTH), lambda i: (i, 0)),
                  pl.BlockSpec((BLOCK, KV_WIDTH), lambda i: (prev_blk(i), 0)),
                  pl.BlockSpec((None, 1, MIX_HEADS), lambda i: (layer, 0, 0))],
        out_specs=pl.BlockSpec((BLOCK, MIX_WIDTH), lambda i: (i, 0)),
        out_shape=jax.ShapeDtypeStruct((t, MIX_WIDTH), BF16),
        compiler_params=_cparams("parallel"),
        name="swa_mixer",
    )(q, k, k, v, v, sinks.reshape(sinks.shape[0], 1, MIX_HEADS))


def _dsa_proj_kernel(x_ref, w_ref, g_ref, wq_ref, wqi_ref, tab_ref,
                     q_out, qi_out, k_out, v_out, ki_out, wi_out, qm_out):
    tab = tab_ref[...]
    c, s1, s2 = tab[0], tab[1], tab[2]
    proj = _dot(x_ref[...].astype(BF16), w_ref[...])
    o1, o2, o3, o4 = Q_RANK, Q_RANK + KV_WIDTH, Q_RANK + 2 * KV_WIDTH, Q_RANK + 2 * KV_WIDTH + MEM_WIDTH
    cq = proj[:, :o1]
    cqn = cq * lax.rsqrt(jnp.mean(cq * cq, axis=-1, keepdims=True) + LN_EPS) * g_ref[...]
    cqn = cqn.astype(BF16)
    q_out[...] = (_rope(_dot(cqn, wq_ref[...]), c, s1, s2) * QK_SCALE).astype(BF16)
    qi_out[...] = (_rope(_dot(cqn, wqi_ref[...]), c, s1, s2) * (IDX_DIM ** -0.5)).astype(BF16)
    k_out[...] = _rope(proj[:, o1:o2], c, s1, s2).astype(BF16)
    v_out[...] = proj[:, o2:o3].astype(BF16)
    qm_out[...] = (proj[:, o3:o4] * QK_SCALE).astype(BF16)
    kiw = proj[:, o4:]
    ki_out[...] = _rope(kiw, c, s1, s2).astype(BF16)
    wi_out[...] = kiw


def _dsa_proj(x, w_in_bf, q_norm_g, w_uq_bf, w_uq_idx_bf, layer, rope_tab, seq):
    t = x.shape[0]
    tm = PROJ_TM
    nts = seq // tm
    qi_w = IDX_HEADS * IDX_DIM
    n_in = w_in_bf.shape[-1]
    row = lambda w: pl.BlockSpec((tm, w), lambda i: (i, 0))
    return pl.pallas_call(
        _dsa_proj_kernel,
        grid=(t // tm,),
        in_specs=[row(D_MODEL),
                  pl.BlockSpec((None, D_MODEL, n_in), lambda i: (layer, 0, 0)),
                  pl.BlockSpec((None, 1, Q_RANK), lambda i: (layer, 0, 0)),
                  pl.BlockSpec((None, Q_RANK, MIX_WIDTH), lambda i: (layer, 0, 0)),
                  pl.BlockSpec((None, Q_RANK, qi_w), lambda i: (layer, 0, 0)),
                  pl.BlockSpec((3, tm, LANES), lambda i: (0, i % nts, 0))],
        out_specs=[row(MIX_WIDTH), row(qi_w), row(KV_WIDTH), row(KV_WIDTH), row(LANES), row(LANES), row(MEM_WIDTH)],
        out_shape=[jax.ShapeDtypeStruct((t, MIX_WIDTH), BF16), jax.ShapeDtypeStruct((t, qi_w), BF16),
                   jax.ShapeDtypeStruct((t, KV_WIDTH), BF16), jax.ShapeDtypeStruct((t, KV_WIDTH), BF16),
                   jax.ShapeDtypeStruct((t, LANES), BF16), jax.ShapeDtypeStruct((t, LANES), F32),
                   jax.ShapeDtypeStruct((t, MEM_WIDTH), BF16)],
        compiler_params=_cparams("parallel"),
        name="dsa_proj",
    )(x, w_in_bf, q_norm_g.reshape(q_norm_g.shape[0], 1, Q_RANK), w_uq_bf, w_uq_idx_bf, rope_tab)


def _lane_fold(x, op, acc):
    for t in range(x.shape[1] // LANES):
        acc = op(acc, x[:, t * LANES:(t + 1) * LANES])
    return acc


def _key_to_float(key):
    return lax.bitcast_convert_type(key ^ (lax.shift_right_arithmetic(key, 31) & 0x7FFFFFFF), F32)


KEY_NEG_INF = INT_MIN + 2 ** 23 - 1


def _dsa_kernel(q_ref, qi_ref, wi_ref, k_ref, v_ref, ki_ref, o_ref, score_ref, s_ref, m_ref, l_ref, acc_ref,
                *, tiles_per_seq, n_sel):
    tq = q_ref.shape[0]
    seq = k_ref.shape[0]
    kc = min(DSA_KC, seq)
    jt = pl.program_id(0) % tiles_per_seq
    n_chunks = (jt * tq + tq + kc - 1) // kc
    qpos = jt * tq + lax.broadcasted_iota(I32, (tq, 1), 0)

    def chunk_start(c):
        return pl.multiple_of(c * kc, kc)

    def key_positions(c):
        return c * kc + lax.broadcasted_iota(I32, (1, kc), 1)

    wi = wi_ref[...][:, IDX_DIM:IDX_DIM + IDX_HEADS] * (IDX_HEADS ** -0.5)
    qi = qi_ref[...]
    qi_heads = [qi[:, h * IDX_DIM:(h + 1) * IDX_DIM] for h in range(IDX_HEADS)]
    wi_heads = [wi[:, h:h + 1] for h in range(IDX_HEADS)]

    def score_chunk(c, carry):
        off = chunk_start(c)
        kch = ki_ref[pl.ds(off, kc), 0:IDX_DIM]
        score = jnp.zeros((tq, kc), F32)
        for h in range(IDX_HEADS):
            score = score + wi_heads[h] * jnp.maximum(_dot_nt(qi_heads[h], kch), 0.0)
        score_ref[:, pl.ds(off, kc)] = jnp.where(key_positions(c) <= qpos, score, -jnp.inf)
        return carry

    lax.fori_loop(0, n_chunks, score_chunk, 0)

    def count(pred):
        def body(c, acc):
            sc = score_ref[:, pl.ds(chunk_start(c), kc)]
            return _lane_fold(jnp.where(pred(sc, c), 1.0, 0.0), jnp.add, acc)
        acc = lax.fori_loop(0, n_chunks, body, jnp.zeros((tq, LANES), F32))
        return jnp.sum(acc, axis=1, keepdims=True)

    def value_bit(it, lo):
        cand = lo + lax.shift_left(jnp.int32(1), 31 - it)
        cand_f = _key_to_float(cand)
        cnt = count(lambda sc, c: sc >= cand_f)
        take = jnp.logical_or(cnt >= n_sel, cand < KEY_NEG_INF)
        return jnp.where(take, cand, lo)

    lo = lax.fori_loop(0, 32, value_bit, jnp.full((tq, 1), INT_MIN, I32))
    thr = _key_to_float(jnp.maximum(lo, KEY_NEG_INF))
    need = n_sel - count(lambda sc, c: sc > thr)
    n_eq = count(lambda sc, c: sc == thr)
    surplus = jnp.where(n_eq > need, jnp.where(lo > KEY_NEG_INF, 1, 0), 0)

    def tie_cut():
        idx_bits = max(seq - 1, 1).bit_length()

        def index_bit(it, cut):
            cand = cut + lax.shift_left(jnp.int32(1), idx_bits - 1 - it)
            cnt = count(lambda sc, c: jnp.logical_and(sc == thr, key_positions(c) < cand))
            return jnp.where(cnt < need, cand, cut)

        cut = lax.fori_loop(0, idx_bits, index_bit, jnp.zeros((tq, 1), I32))
        return jnp.where(surplus > 0, cut, seq)

    cut = lax.cond(jnp.max(surplus) > 0, tie_cut, lambda: jnp.full((tq, 1), seq, I32))

    def bias_chunk(c, carry):
        off = chunk_start(c)
        sc = score_ref[:, pl.ds(off, kc)]
        kpos = key_positions(c)
        tie = jnp.logical_and(sc == thr, kpos <= cut)
        sel = jnp.logical_and(kpos <= qpos, jnp.logical_or(sc > thr, tie))
        score_ref[:, pl.ds(off, kc)] = jnp.where(sel, 0.0, NEG_BIG)
        return carry

    lax.fori_loop(0, n_chunks, bias_chunk, 0)

    q = q_ref[...]
    for g in range(KV_HEADS):
        heads = range(g * HEADS_PER_KV, (g + 1) * HEADS_PER_KV)
        qg = jnp.concatenate([q[:, h * HEAD_DIM:(h + 1) * HEAD_DIM] for h in heads], axis=0)
        kv_cols = slice(g * HEAD_DIM, (g + 1) * HEAD_DIM)
        m_ref[...] = jnp.full(m_ref.shape, NEG_BIG, F32)
        l_ref[...] = jnp.zeros(l_ref.shape, F32)
        acc_ref[...] = jnp.zeros(acc_ref.shape, F32)

        def logits_chunk(c, carry, qg=qg, kv_cols=kv_cols):
            off = chunk_start(c)
            bias = score_ref[:, pl.ds(off, kc)]
            s = _dot_nt(qg, k_ref[pl.ds(off, kc), kv_cols]) + jnp.concatenate([bias] * HEADS_PER_KV, axis=0)
            s_ref[:, pl.ds(off, kc)] = s
            m_ref[...] = _lane_fold(s, jnp.maximum, m_ref[...])
            return carry

        lax.fori_loop(0, n_chunks, logits_chunk, 0)
        m = jnp.max(m_ref[...], axis=1, keepdims=True)

        def pv_chunk(c, carry, m=m, kv_cols=kv_cols):
            off = chunk_start(c)
            p = jnp.exp(s_ref[:, pl.ds(off, kc)] - m)
            l_ref[...] = _lane_fold(p, jnp.add, l_ref[...])
            acc_ref[...] += _dot(p.astype(BF16), v_ref[pl.ds(off, kc), kv_cols])
            return carry

        lax.fori_loop(0, n_chunks, pv_chunk, 0)
        out = acc_ref[...] / jnp.sum(l_ref[...], axis=1, keepdims=True)
        for r, h in enumerate(heads):
            o_ref[:, h * HEAD_DIM:(h + 1) * HEAD_DIM] = out[r * tq:(r + 1) * tq].astype(o_ref.dtype)


def _dsa_attention(q, qi, wi, k, v, ki, seq):
    t = q.shape[0]
    tq = DSA_TQ
    nts = seq // tq
    n_sel = min(TOPK_MAX, seq // 4)
    per_seq = lambda w: pl.BlockSpec((seq, w), lambda i: (i // nts, 0))
    return pl.pallas_call(
        functools.partial(_dsa_kernel, tiles_per_seq=nts, n_sel=n_sel),
        grid=(t // tq,),
        in_specs=[pl.BlockSpec((tq, MIX_WIDTH), lambda i: (i, 0)),
                  pl.BlockSpec((tq, IDX_HEADS * IDX_DIM), lambda i: (i, 0)),
                  pl.BlockSpec((tq, LANES), lambda i: (i, 0)),
                  per_seq(KV_WIDTH), per_seq(KV_WIDTH), per_seq(LANES)],
        out_specs=pl.BlockSpec((tq, MIX_WIDTH), lambda i: (i, 0)),
        out_shape=jax.ShapeDtypeStruct((t, MIX_WIDTH), BF16),
        scratch_shapes=[pltpu.VMEM((tq, seq), F32),
                        pltpu.VMEM((HEADS_PER_KV * tq, seq), F32),
                        pltpu.VMEM((HEADS_PER_KV * tq, LANES), F32),
                        pltpu.VMEM((HEADS_PER_KV * tq, LANES), F32),
                        pltpu.VMEM((HEADS_PER_KV * tq, HEAD_DIM), F32)],
        compiler_params=_cparams("parallel"),
        name="dsa_attention",
    )(q, qi, wi, k, v, ki)


def _mem_attn_kernel(q_ref, k_ref, v_ref, o_ref):
    q = q_ref[...]
    for h in range(MEM_HEADS):
        sl = slice(h * HEAD_DIM, (h + 1) * HEAD_DIM)
        logits = _dot_nt(q[:, sl], k_ref[:, sl])
        e = jnp.exp(logits - jnp.max(logits, axis=-1, keepdims=True))
        p = e / jnp.sum(e, axis=-1, keepdims=True)
        o_ref[:, sl] = _dot(p.astype(BF16), v_ref[:, sl]).astype(o_ref.dtype)


def _mem_attention(qm, kv_mem, seq, mem_len):
    t = qm.shape[0]
    tq = MEM_TQ
    nqs = seq // tq
    return pl.pallas_call(
        _mem_attn_kernel,
        grid=(t // tq,),
        in_specs=[pl.BlockSpec((tq, MEM_WIDTH), lambda i: (i, 0)),
                  pl.BlockSpec((mem_len, MEM_WIDTH), lambda i: (i // nqs, 0)),
                  pl.BlockSpec((mem_len, MEM_WIDTH), lambda i: (i // nqs, 1))],
        out_specs=pl.BlockSpec((tq, MEM_WIDTH), lambda i: (i, 0)),
        out_shape=jax.ShapeDtypeStruct((t, MEM_WIDTH), BF16),
        compiler_params=_cparams("parallel"),
        name="mem_attention",
    )(qm, kv_mem, kv_mem)


def _outproj_ln_kernel(x_ref, mix_ref, mo_ref, w_ref, g_ref, b_ref, o_ref, wbf_ref):
    @pl.when(pl.program_id(0) == 0)
    def _():
        wbf_ref[...] = w_ref[...].astype(BF16)

    y = _dot(mix_ref[...], wbf_ref[0:MIX_WIDTH, :]) + _dot(mo_ref[...], wbf_ref[MIX_WIDTH:, :])
    o_ref[...] = _layer_norm(ALPHA * x_ref[...] + y, g_ref[...], b_ref[...])


def _outproj_ln(x, mix, mem_out, w_out, ln_g, ln_b, layer):
    t = x.shape[0]
    tm = LN_TM
    vec = lambda: pl.BlockSpec((None, 1, D_MODEL), lambda i: (layer, 0, 0))
    return pl.pallas_call(
        _outproj_ln_kernel,
        grid=(t // tm,),
        in_specs=[pl.BlockSpec((tm, D_MODEL), lambda i: (i, 0)),
                  pl.BlockSpec((tm, MIX_WIDTH), lambda i: (i, 0)),
                  pl.BlockSpec((tm, MEM_WIDTH), lambda i: (i, 0)),
                  pl.BlockSpec((None, D_MODEL, D_MODEL), lambda i: (layer, 0, 0)),
                  vec(), vec()],
        out_specs=pl.BlockSpec((tm, D_MODEL), lambda i: (i, 0)),
        out_shape=jax.ShapeDtypeStruct((t, D_MODEL), F32),
        scratch_shapes=[pltpu.VMEM((D_MODEL, D_MODEL), BF16)],
        compiler_params=_cparams("arbitrary"),
        name="outproj_ln",
    )(x, mix, mem_out, w_out, ln_g.reshape(DEPTH, 1, D_MODEL), ln_b.reshape(DEPTH, 1, D_MODEL))


def _router_kernel(x_ref, w_ref, b_ref, idx_ref, gate_ref):
    logits = jnp.dot(x_ref[...], w_ref[...], preferred_element_type=F32,
                     precision=lax.Precision.HIGHEST) + b_ref[...]
    lane = lax.broadcasted_iota(I32, logits.shape, 1).astype(F32)
    vals, idxs = [], []
    for _ in range(TOP_K):
        m = jnp.max(logits, axis=-1, keepdims=True)
        ix = jnp.min(jnp.where(logits == m, lane, float(N_EXPERTS)), axis=-1, keepdims=True)
        vals.append(m)
        idxs.append(ix)
        logits = jnp.where(lane == ix, -jnp.inf, logits)
    e = jnp.exp(jnp.concatenate(vals, axis=1) - vals[0])
    idx_ref[...] = jnp.concatenate(idxs, axis=1).astype(I32)
    gate_ref[...] = e / jnp.sum(e, axis=-1, keepdims=True)


def _router(x, router_w, router_b, layer):
    t = x.shape[0]
    tm = ROUTER_TM
    return pl.pallas_call(
        _router_kernel,
        grid=(t // tm,),
        in_specs=[pl.BlockSpec((tm, D_MODEL), lambda i: (i, 0)),
                  pl.BlockSpec((None, D_MODEL, N_EXPERTS), lambda i: (layer, 0, 0)),
                  pl.BlockSpec((None, 1, N_EXPERTS), lambda i: (layer, 0, 0))],
        out_specs=[pl.BlockSpec((tm, TOP_K), lambda i: (i, 0)), pl.BlockSpec((tm, TOP_K), lambda i: (i, 0))],
        out_shape=[jax.ShapeDtypeStruct((t, TOP_K), I32), jax.ShapeDtypeStruct((t, TOP_K), F32)],
        compiler_params=_cparams("parallel"),
        name="router",
    )(x, router_w, router_b.reshape(DEPTH, 1, N_EXPERTS))


def _dispatch_plan(top_idx):
    t = top_idx.shape[0]
    n_assign = t * TOP_K
    tm = MOE_TM
    n_tiles = n_assign // tm + N_EXPERTS
    e = top_idx.reshape(n_assign)
    experts = jnp.arange(N_EXPERTS, dtype=I32)
    onehot = (e[:, None] == experts[None, :]).astype(I32)
    csum = jnp.cumsum(onehot, axis=0)
    counts = csum[-1]
    padded = ((counts + tm - 1) // tm) * tm
    group_end = jnp.cumsum(padded)
    group_start = group_end - padded
    slot = jnp.sum((csum - onehot + group_start[None, :]) * onehot, axis=1)
    n_used = group_end[-1] // tm
    tile = jnp.arange(n_tiles, dtype=I32)
    tile_expert = jnp.minimum(jnp.sum((tile[:, None] * tm >= group_end[None, :]).astype(I32), axis=1), N_EXPERTS - 1)
    last_expert = jnp.sum(jnp.where(tile == n_used - 1, tile_expert, 0))
    tile_expert = jnp.where(tile < n_used, tile_expert, last_expert).astype(I32)
    order = jnp.sort(e * n_assign + jnp.arange(n_assign, dtype=I32)) % n_assign
    order = jnp.concatenate([order // TOP_K, jnp.zeros((tm,), I32)])
    unpadded_start = jnp.cumsum(counts) - counts
    window = unpadded_start[tile_expert] + tile * tm - group_start[tile_expert]
    window = jnp.where(tile < n_used, window, n_assign)
    row_ids = jax.vmap(lambda w: lax.dynamic_slice(order, (w,), (tm,)))(window)
    nonempty = counts > 0
    later = jnp.where(nonempty[None, :] & (experts[None, :] > experts[:, None]), experts[None, :], N_EXPERTS)
    next_expert = jnp.min(later, axis=1)
    next_expert = jnp.where(next_expert < N_EXPERTS, next_expert, -1)
    group_parity = (jnp.cumsum(nonempty.astype(I32)) - 1) % 2
    info = jnp.stack([tile_expert, next_expert[tile_expert], group_parity[tile_expert]]).astype(I32)
    return row_ids.reshape(n_tiles, 1, tm), info, n_used.reshape(1).astype(I32), slot.reshape(t, TOP_K)


def _gather_rows(src_hbm, ids_ref, n_rows, dst_ref, sem):
    def body(r, carry):
        pltpu.make_async_copy(src_hbm.at[pl.ds(ids_ref[0, 0, r], 1)], dst_ref.at[pl.ds(r, 1)], sem).start()
        return carry
    lax.fori_loop(0, n_rows, body, 0, unroll=8)


def _wait_rows(dst_ref, sem):
    pltpu.make_async_copy(dst_ref, dst_ref, sem).wait()


def _moe_kernel(info_ref, nu_ref, ids_ref, ids_next_ref, x_hbm, wgu_hbm, bgu_ref, wd_hbm, bd_ref, o_ref,
                xbuf0, xbuf1, sems, wgu_f32, wd_f32, wsems, wgu_bf, wd_bf, *, layer):
    j = pl.program_id(0)
    n_used = nu_ref[0]
    tm = xbuf0.shape[0]
    bufs = ((xbuf0, sems.at[0]), (xbuf1, sems.at[1]))
    expert, next_expert, wslot = info_ref[0, j], info_ref[1, j], info_ref[2, j]

    def weight_copies(e, slot):
        return (pltpu.make_async_copy(wgu_hbm.at[layer, e], wgu_f32.at[slot], wsems.at[0, slot]),
                pltpu.make_async_copy(wd_hbm.at[layer, e], wd_f32.at[slot], wsems.at[1, slot]))

    @pl.when(j == 0)
    def _():
        for cp in weight_copies(expert, wslot):
            cp.start()
        _gather_rows(x_hbm, ids_ref, tm, xbuf0, sems.at[0])

    def expert_tile(cur, nxt):
        _wait_rows(*cur)

        @pl.when(jnp.logical_or(j == 0, expert != info_ref[0, jnp.maximum(j - 1, 0)]))
        def _():
            for cp in weight_copies(expert, wslot):
                cp.wait()

            @pl.when(next_expert >= 0)
            def _():
                for cp in weight_copies(next_expert, 1 - wslot):
                    cp.start()

            wgu_bf[...] = wgu_f32[wslot].astype(BF16)
            wd_bf[...] = wd_f32[wslot].astype(BF16)

        for r in range(tm):
            pltpu.make_async_copy(x_hbm.at[pl.ds(ids_next_ref[0, 0, r], 1)], nxt[0].at[pl.ds(r, 1)],
                                  nxt[1]).start(priority=r % 2)
        h = _dot(cur[0][...].astype(BF16), wgu_bf[...]) + bgu_ref[...]
        gate = jnp.minimum(h[:, :D_EXPERT], SWIGLU_LIMIT)
        up = jnp.clip(h[:, D_EXPERT:], -SWIGLU_LIMIT, SWIGLU_LIMIT)
        act = (up + 1.0) * gate * jax.nn.sigmoid(SWIGLU_ALPHA * gate)
        o_ref[...] = _dot(act.astype(BF16), wd_bf[...]) + bd_ref[...]

    for parity in range(2):
        @pl.when(jnp.logical_and(j < n_used, j % 2 == parity))
        def _(parity=parity):
            expert_tile(bufs[parity], bufs[1 - parity])

        @pl.when(jnp.logical_and(j == n_used, j % 2 == parity))
        def _(parity=parity):
            _wait_rows(*bufs[parity])

    @pl.when(j >= n_used)
    def _():
        o_ref[...] = jnp.zeros_like(o_ref)


def _moe_experts(x, row_ids, tile_info, n_used, w_gate_up, b_gate_up, w_down, b_down, layer):
    n_tiles = row_ids.shape[0]
    tm = MOE_TM
    grid_spec = pltpu.PrefetchScalarGridSpec(
        num_scalar_prefetch=2,
        grid=(n_tiles,),
        in_specs=[pl.BlockSpec((1, 1, tm), lambda j, info, nu: (j, 0, 0), memory_space=pltpu.SMEM),
                  pl.BlockSpec((1, 1, tm), lambda j, info, nu: (jnp.minimum(j + 1, n_tiles - 1), 0, 0),
                               memory_space=pltpu.SMEM),
                  pl.BlockSpec(memory_space=pl.ANY),
                  pl.BlockSpec(memory_space=pl.ANY),
                  pl.BlockSpec((None, None, 1, 2 * D_EXPERT), lambda j, info, nu: (layer, info[0, j], 0, 0)),
                  pl.BlockSpec(memory_space=pl.ANY),
                  pl.BlockSpec((None, None, 1, D_MODEL), lambda j, info, nu: (layer, info[0, j], 0, 0))],
        out_specs=pl.BlockSpec((tm, D_MODEL), lambda j, info, nu: (j, 0)),
        scratch_shapes=[pltpu.VMEM((tm, D_MODEL), F32), pltpu.VMEM((tm, D_MODEL), F32),
                        pltpu.SemaphoreType.DMA((2,)),
                        pltpu.VMEM((2, D_MODEL, 2 * D_EXPERT), F32), pltpu.VMEM((2, D_EXPERT, D_MODEL), F32),
                        pltpu.SemaphoreType.DMA((2, 2)),
                        pltpu.VMEM((D_MODEL, 2 * D_EXPERT), BF16), pltpu.VMEM((D_EXPERT, D_MODEL), BF16)],
    )
    return pl.pallas_call(
        functools.partial(_moe_kernel, layer=layer),
        grid_spec=grid_spec,
        out_shape=jax.ShapeDtypeStruct((n_tiles * tm, D_MODEL), F32),
        compiler_params=_cparams("arbitrary"),
        name="moe_experts",
    )(tile_info, n_used, row_ids, row_ids, x, w_gate_up,
      b_gate_up.reshape(DEPTH, N_EXPERTS, 1, 2 * D_EXPERT), w_down, b_down.reshape(DEPTH, N_EXPERTS, 1, D_MODEL))


def _combine_ln_kernel(slot_ref, slot_next_ref, ys_hbm, gate_ref, x_ref, g_ref, b_ref, o_ref, buf0, buf1, sems,
                       *, n_steps):
    i = pl.program_id(0)
    rows = buf0.shape[0]
    tc = rows // TOP_K
    bufs = ((buf0, sems.at[0]), (buf1, sems.at[1]))

    @pl.when(i == 0)
    def _():
        _gather_rows(ys_hbm, slot_ref, rows, buf0, sems.at[0])

    def token_tile(cur, nxt):
        _wait_rows(*cur)
        if nxt is not None:
            for r in range(rows):
                pltpu.make_async_copy(ys_hbm.at[pl.ds(slot_next_ref[0, 0, r], 1)], nxt[0].at[pl.ds(r, 1)],
                                      nxt[1]).start(priority=r % 2)
        gates = gate_ref[...]
        f = gates[:, 0:1] * cur[0][0:tc, :]
        for k in range(1, TOP_K):
            f = f + gates[:, k:k + 1] * cur[0][k * tc:(k + 1) * tc, :]
        o_ref[...] = _layer_norm(ALPHA * x_ref[...] + f, g_ref[...], b_ref[...])

    for parity in range(2):
        @pl.when(jnp.logical_and(i < n_steps - 1, i % 2 == parity))
        def _(parity=parity):
            token_tile(bufs[parity], bufs[1 - parity])

    @pl.when(i == n_steps - 1)
    def _():
        token_tile(bufs[(n_steps - 1) % 2], None)


def _combine_ln(x, ys, slots, gates, ln_g, ln_b, layer):
    t = x.shape[0]
    tc = COMB_TC
    n = t // tc
    slots_km = slots.reshape(n, tc, TOP_K).transpose(0, 2, 1).reshape(n, 1, TOP_K * tc)
    vec = lambda: pl.BlockSpec((None, 1, D_MODEL), lambda i: (layer, 0, 0))
    return pl.pallas_call(
        functools.partial(_combine_ln_kernel, n_steps=n),
        grid=(n,),
        in_specs=[pl.BlockSpec((1, 1, TOP_K * tc), lambda i: (i, 0, 0), memory_space=pltpu.SMEM),
                  pl.BlockSpec((1, 1, TOP_K * tc), lambda i: (jnp.minimum(i + 1, n - 1), 0, 0),
                               memory_space=pltpu.SMEM),
                  pl.BlockSpec(memory_space=pl.ANY),
                  pl.BlockSpec((tc, TOP_K), lambda i: (i, 0)),
                  pl.BlockSpec((tc, D_MODEL), lambda i: (i, 0)),
                  vec(), vec()],
        out_specs=pl.BlockSpec((tc, D_MODEL), lambda i: (i, 0)),
        out_shape=jax.ShapeDtypeStruct((t, D_MODEL), F32),
        scratch_shapes=[pltpu.VMEM((TOP_K * tc, D_MODEL), F32), pltpu.VMEM((TOP_K * tc, D_MODEL), F32),
                        pltpu.SemaphoreType.DMA((2,))],
        compiler_params=_cparams("arbitrary"),
        name="combine_ln",
    )(slots_km, slots_km, ys, gates, x, ln_g.reshape(DEPTH, 1, D_MODEL), ln_b.reshape(DEPTH, 1, D_MODEL))


def _rope_tables(seq):
    inv = ROPE_THETA ** (-jnp.arange(0, ROT_DIM, 2, dtype=F32) / ROT_DIM)
    ang = jnp.arange(seq, dtype=F32)[:, None] * inv[None, :]
    cos, sin = jnp.cos(ang), jnp.sin(ang)
    half = ROT_DIM // 2
    rest = HEAD_DIM - ROT_DIM
    zeros_h = jnp.zeros((seq, half), F32)
    head = lambda a, b, fill: jnp.concatenate([a, b, jnp.full((seq, rest), fill, F32)], axis=1)
    tab = jnp.stack([head(cos, cos, 1.0), head(zeros_h, sin, 0.0), head(-sin, zeros_h, 0.0)])
    return jnp.tile(tab, (1, 1, LANES // HEAD_DIM))


def kernel(x, mem, a_w_in, a_conv_w, b_w_in, b_q_norm_g, b_w_uq, b_w_uq_idx, c_w_in, c_sinks, w_out, w_kv_mem,
           ln1_g, ln1_b, router_w, router_b, w_gate_up, b_gate_up, w_down, b_down, ln2_g, ln2_b):
    batch, seq, _ = x.shape
    mem_len = mem.shape[1]
    t = batch * seq
    xs = x.reshape(t, D_MODEL)
    mem2 = mem.reshape(batch * mem_len, D_MODEL)
    rope_tab = _rope_tables(seq)

    o1 = Q_RANK
    o3 = o1 + 2 * KV_WIDTH
    o5 = o3 + IDX_DIM + IDX_HEADS
    b_w = jnp.concatenate([b_w_in[:, :, :o3], b_w_in[:, :, o5:], b_w_in[:, :, o3:o5],
                           jnp.zeros(b_w_in.shape[:2] + (LANES - IDX_DIM - IDX_HEADS,), b_w_in.dtype)], axis=2)
    a_w, b_w, c_w = a_w_in.astype(BF16), b_w.astype(BF16), c_w_in.astype(BF16)
    b_wq, b_wqi = b_w_uq.astype(BF16), b_w_uq_idx.astype(BF16)

    for i in range(DEPTH):
        kind, j = i % N_MIXERS, i // N_MIXERS
        if kind == 0:
            mix, qm = _conv_mixer(xs, a_w, a_conv_w, j, seq)
        elif kind == 1:
            q, qi, k, v, ki, wi, qm = _dsa_proj(xs, b_w, b_q_norm_g, b_wq, b_wqi, j, rope_tab, seq)
            mix = _dsa_attention(q, qi, wi, k, v, ki, seq)
        else:
            q, k, v, qm = _swa_proj(xs, c_w, j, rope_tab, seq)
            mix = _swa_mixer(q, k, v, c_sinks, j, seq)
        kv_mem = _matmul(mem2, w_kv_mem, i, BF16, min(MM_TM, batch * mem_len), MM_TN)
        mem_out = _mem_attention(qm, kv_mem, seq, mem_len)
        xs = _outproj_ln(xs, mix, mem_out, w_out, ln1_g, ln1_b, i)
        top_idx, gates = _router(xs, router_w, router_b, i)
        row_ids, tile_info, n_used, slots = _dispatch_plan(top_idx)
        ys = _moe_experts(xs, row_ids, tile_info, n_used, w_gate_up, b_gate_up, w_down, b_down, i)
        xs = _combine_ln(xs, ys, slots, gates, ln2_g, ln2_b, i)
    return xs.reshape(batch, seq, D_MODEL)
```

```python
import functools

import jax
import jax.numpy as jnp
from jax import lax
from jax.experimental import pallas as pl
from jax.experimental.pallas import tpu as pltpu

F32 = jnp.float32
BF16 = jnp.bfloat16
I32 = jnp.int32

D_MODEL = 1024
DEPTH = 4
N_MIXERS = 3
HEAD_DIM = 64
MIX_HEADS = 12
MIX_WIDTH = MIX_HEADS * HEAD_DIM
MEM_HEADS = 4
MEM_WIDTH = MEM_HEADS * HEAD_DIM
KV_HEADS = 4
KV_WIDTH = KV_HEADS * HEAD_DIM
HEADS_PER_KV = MIX_HEADS // KV_HEADS
ROT_DIM = HEAD_DIM // 4
ROPE_THETA = 500000.0
CONV_WIDTH = 3
Q_RANK = 256
IDX_HEADS = 8
IDX_DIM = 64
TOPK_MAX = 256
BLOCK = 128
N_EXPERTS = 32
TOP_K = 4
D_EXPERT = 1024
SWIGLU_LIMIT = 7.0
SWIGLU_ALPHA = 1.702
ALPHA = (2 * DEPTH) ** 0.25
LN_EPS = 1e-5

LANES = 128
QK_SCALE = HEAD_DIM ** -0.5
INT_MIN = -(2 ** 31)
NEG_BIG = -1e30
VMEM_LIMIT = 48 * 1024 * 1024

MM_TM, MM_TN = 1024, 512
PROJ_TM = 512
CONV_TS = 512
MEM_TQ = 512
LN_TM = 512
ROUTER_TM = 512
MOE_TM = 256
COMB_TC = 128
DSA_TQ = 256
DSA_KC = 512
PREP_TM = 256


def _cparams(*sem):
    return pltpu.CompilerParams(dimension_semantics=sem, vmem_limit_bytes=VMEM_LIMIT)


def _dot(a, b):
    return jnp.dot(a, b, preferred_element_type=F32)


def _dot_nt(a, b):
    return lax.dot_general(a, b, (((1,), (1,)), ((), ())), preferred_element_type=F32)


def _layer_norm(h, g, b):
    mu = jnp.mean(h, axis=-1, keepdims=True)
    d = h - mu
    var = jnp.mean(d * d, axis=-1, keepdims=True)
    return d * lax.rsqrt(var + LN_EPS) * g + b


def _rope(x, c, s1, s2):
    outs = []
    for k in range(x.shape[1] // LANES):
        xs = x[:, k * LANES:(k + 1) * LANES]
        outs.append(xs * c + pltpu.roll(xs, 8, 1) * s1 + pltpu.roll(xs, LANES - 8, 1) * s2)
    return outs[0] if len(outs) == 1 else jnp.concatenate(outs, axis=1)


def _matmul_kernel(x_ref, w_ref, o_ref):
    o_ref[...] = _dot(x_ref[...].astype(BF16), w_ref[...].astype(BF16)).astype(o_ref.dtype)


def _matmul(x, w, layer, out_dtype, tm, tn):
    m, k = x.shape
    n = w.shape[-1]
    return pl.pallas_call(
        _matmul_kernel,
        grid=(m // tm, n // tn),
        in_specs=[pl.BlockSpec((tm, k), lambda i, j: (i, 0)),
                  pl.BlockSpec((None, k, tn), lambda i, j: (layer, 0, j))],
        out_specs=pl.BlockSpec((tm, tn), lambda i, j: (i, j)),
        out_shape=jax.ShapeDtypeStruct((m, n), out_dtype),
        compiler_params=_cparams("parallel", "parallel"),
        name="proj_matmul",
    )(x, w)


def _conv_kernel(x_ref, w_ref, cw_ref, mix_ref, qm_ref, zz_ref, *, tiles_per_seq):
    ts = x_ref.shape[0]
    wb = MIX_WIDTH

    @pl.when(pl.program_id(0) % tiles_per_seq == 0)
    def _():
        zz_ref[0:8, :] = jnp.zeros((8, wb), F32)

    proj = _dot(x_ref[...].astype(BF16), w_ref[...])
    z = proj[:, 2 * wb:3 * wb] * proj[:, 0:wb]
    zz_ref[8:, :] = z
    cw = cw_ref[...]
    conv = cw[0:1, :] * zz_ref[6:6 + ts, :] + cw[1:2, :] * zz_ref[7:7 + ts, :] + cw[2:3, :] * z
    mix_ref[...] = (proj[:, wb:2 * wb] * conv).astype(mix_ref.dtype)
    qm_ref[...] = (proj[:, 3 * wb:] * QK_SCALE).astype(qm_ref.dtype)
    zz_ref[0:8, :] = zz_ref[ts:ts + 8, :]


def _conv_mixer(x, w_in_bf, conv_w, layer, seq):
    t = x.shape[0]
    ts = CONV_TS
    n_in = w_in_bf.shape[-1]
    return pl.pallas_call(
        functools.partial(_conv_kernel, tiles_per_seq=seq // ts),
        grid=(t // ts,),
        in_specs=[pl.BlockSpec((ts, D_MODEL), lambda i: (i, 0)),
                  pl.BlockSpec((None, D_MODEL, n_in), lambda i: (layer, 0, 0)),
                  pl.BlockSpec((None, CONV_WIDTH, MIX_WIDTH), lambda i: (layer, 0, 0))],
        out_specs=[pl.BlockSpec((ts, MIX_WIDTH), lambda i: (i, 0)), pl.BlockSpec((ts, MEM_WIDTH), lambda i: (i, 0))],
        out_shape=[jax.ShapeDtypeStruct((t, MIX_WIDTH), BF16), jax.ShapeDtypeStruct((t, MEM_WIDTH), BF16)],
        scratch_shapes=[pltpu.VMEM((ts + 8, MIX_WIDTH), F32)],
        compiler_params=_cparams("arbitrary"),
        name="conv_mixer",
    )(x, w_in_bf, conv_w)


def _swa_proj_kernel(x_ref, w_ref, tab_ref, q_out, k_out, v_out, qm_out):
    tab = tab_ref[...]
    c, s1, s2 = tab[0], tab[1], tab[2]
    proj = _dot(x_ref[...].astype(BF16), w_ref[...])
    o1, o2, o3 = MIX_WIDTH, MIX_WIDTH + KV_WIDTH, MIX_WIDTH + 2 * KV_WIDTH
    q_out[...] = (_rope(proj[:, :o1], c, s1, s2) * QK_SCALE).astype(BF16)
    k_out[...] = _rope(proj[:, o1:o2], c, s1, s2).astype(BF16)
    v_out[...] = proj[:, o2:o3].astype(BF16)
    qm_out[...] = (proj[:, o3:] * QK_SCALE).astype(BF16)


def _swa_proj(x, w_in_bf, layer, rope_tab, seq):
    t = x.shape[0]
    tm = PROJ_TM
    nts = seq // tm
    n_in = w_in_bf.shape[-1]
    row = lambda w: pl.BlockSpec((tm, w), lambda i: (i, 0))
    return pl.pallas_call(
        _swa_proj_kernel,
        grid=(t // tm,),
        in_specs=[row(D_MODEL),
                  pl.BlockSpec((None, D_MODEL, n_in), lambda i: (layer, 0, 0)),
                  pl.BlockSpec((3, tm, LANES), lambda i: (0, i % nts, 0))],
        out_specs=[row(MIX_WIDTH), row(KV_WIDTH), row(KV_WIDTH), row(MEM_WIDTH)],
        out_shape=[jax.ShapeDtypeStruct((t, MIX_WIDTH), BF16), jax.ShapeDtypeStruct((t, KV_WIDTH), BF16),
                   jax.ShapeDtypeStruct((t, KV_WIDTH), BF16), jax.ShapeDtypeStruct((t, MEM_WIDTH), BF16)],
        compiler_params=_cparams("parallel"),
        name="swa_proj",
    )(x, w_in_bf, rope_tab)


def _swa_kernel(q_ref, kc_ref, kp_ref, vc_ref, vp_ref, sink_ref, o_ref, *, blocks_per_seq):
    first = (pl.program_id(0) % blocks_per_seq) == 0
    q = q_ref[...]
    k = jnp.concatenate([kp_ref[...], kc_ref[...]], axis=0)
    v = jnp.concatenate([vp_ref[...], vc_ref[...]], axis=0)
    qi = lax.broadcasted_iota(I32, (BLOCK, 2 * BLOCK), 0)
    r = lax.broadcasted_iota(I32, (BLOCK, 2 * BLOCK), 1)
    keep = jnp.where(r > qi, jnp.where(r <= qi + BLOCK, 1, 0), 0)
    keep = jnp.where(first, jnp.where(r >= BLOCK, keep, 0), keep)
    bias = jnp.where(keep > 0, 0.0, -jnp.inf)
    bias = jnp.concatenate([bias] * HEADS_PER_KV, axis=0)
    sinks = sink_ref[...]
    for g in range(KV_HEADS):
        heads = range(g * HEADS_PER_KV, (g + 1) * HEADS_PER_KV)
        kv_cols = slice(g * HEAD_DIM, (g + 1) * HEAD_DIM)
        qg = jnp.concatenate([q[:, h * HEAD_DIM:(h + 1) * HEAD_DIM] for h in heads], axis=0)
        sink = jnp.concatenate([jnp.broadcast_to(sinks[0:1, h:h + 1], (BLOCK, 1)) for h in heads], axis=0)
        logits = _dot_nt(qg, k[:, kv_cols]) + bias
        m = jnp.maximum(jnp.max(logits, axis=-1, keepdims=True), sink)
        e = jnp.exp(logits - m)
        p = e / (jnp.sum(e, axis=-1, keepdims=True) + jnp.exp(sink - m))
        out = _dot(p.astype(BF16), v[:, kv_cols])
        for n, h in enumerate(heads):
            o_ref[:, h * HEAD_DIM:(h + 1) * HEAD_DIM] = out[n * BLOCK:(n + 1) * BLOCK].astype(o_ref.dtype)


def _swa_mixer(q, k, v, sinks, layer, seq):
    t = q.shape[0]
    nbs = seq // BLOCK
    prev_blk = lambda i: jnp.maximum(i - 1, 0)
    return pl.pallas_call(
        functools.partial(_swa_kernel, blocks_per_seq=nbs),
        grid=(t // BLOCK,),
        in_specs=[pl.BlockSpec((BLOCK, MIX_WIDTH), lambda i: (i, 0)),
                  pl.BlockSpec((BLOCK, KV_WIDTH), lambda i: (i, 0)),
                  pl.BlockSpec((BLOCK, KV_WIDTH), lambda i: (prev_blk(i), 0)),
                  pl.BlockSpec((BLOCK, KV_WIDTH), lambda i: (i, 0)),
                  pl.BlockSpec((BLOCK, KV_WIDTH), lambda i: (prev_blk(i), 0)),
                  pl.BlockSpec((None, 1, MIX_HEADS), lambda i: (layer, 0, 0))],
        out_specs=pl.BlockSpec((BLOCK, MIX_WIDTH), lambda i: (i, 0)),
        out_shape=jax.ShapeDtypeStruct((t, MIX_WIDTH), BF16),
        compiler_params=_cparams("parallel"),
        name="swa_mixer",
    )(q, k, k, v, v, sinks.reshape(sinks.shape[0], 1, MIX_HEADS))


def _dsa_proj_kernel(x_ref, w_ref, g_ref, wq_ref, wqi_ref, tab_ref,
                     q_out, qi_out, k_out, v_out, ki_out, wi_out, qm_out):
    tab = tab_ref[...]
    c, s1, s2 = tab[0], tab[1], tab[2]
    proj = _dot(x_ref[...].astype(BF16), w_ref[...])
    o1, o2, o3, o4 = Q_RANK, Q_RANK + KV_WIDTH, Q_RANK + 2 * KV_WIDTH, Q_RANK + 2 * KV_WIDTH + MEM_WIDTH
    cq = proj[:, :o1]
    cqn = cq * lax.rsqrt(jnp.mean(cq * cq, axis=-1, keepdims=True) + LN_EPS) * g_ref[...]
    cqn = cqn.astype(BF16)
    q_out[...] = (_rope(_dot(cqn, wq_ref[...]), c, s1, s2) * QK_SCALE).astype(BF16)
    qi_out[...] = (_rope(_dot(cqn, wqi_ref[...]), c, s1, s2) * (IDX_DIM ** -0.5)).astype(BF16)
    k_out[...] = _rope(proj[:, o1:o2], c, s1, s2).astype(BF16)
    v_out[...] = proj[:, o2:o3].astype(BF16)
    qm_out[...] = (proj[:, o3:o4] * QK_SCALE).astype(BF16)
    kiw = proj[:, o4:]
    ki_out[...] = _rope(kiw, c, s1, s2).astype(BF16)
    wi_out[...] = kiw


def _dsa_proj(x, w_in_bf, q_norm_g, w_uq_bf, w_uq_idx_bf, layer, rope_tab, seq):
    t = x.shape[0]
    tm = PROJ_TM
    nts = seq // tm
    qi_w = IDX_HEADS * IDX_DIM
    n_in = w_in_bf.shape[-1]
    row = lambda w: pl.BlockSpec((tm, w), lambda i: (i, 0))
    return pl.pallas_call(
        _dsa_proj_kernel,
        grid=(t // tm,),
        in_specs=[row(D_MODEL),
                  pl.BlockSpec((None, D_MODEL, n_in), lambda i: (layer, 0, 0)),
                  pl.BlockSpec((None, 1, Q_RANK), lambda i: (layer, 0, 0)),
                  pl.BlockSpec((None, Q_RANK, MIX_WIDTH), lambda i: (layer, 0, 0)),
                  pl.BlockSpec((None, Q_RANK, qi_w), lambda i: (layer, 0, 0)),
                  pl.BlockSpec((3, tm, LANES), lambda i: (0, i % nts, 0))],
        out_specs=[row(MIX_WIDTH), row(qi_w), row(KV_WIDTH), row(KV_WIDTH), row(LANES), row(LANES), row(MEM_WIDTH)],
        out_shape=[jax.ShapeDtypeStruct((t, MIX_WIDTH), BF16), jax.ShapeDtypeStruct((t, qi_w), BF16),
                   jax.ShapeDtypeStruct((t, KV_WIDTH), BF16), jax.ShapeDtypeStruct((t, KV_WIDTH), BF16),
                   jax.ShapeDtypeStruct((t, LANES), BF16), jax.ShapeDtypeStruct((t, LANES), F32),
                   jax.ShapeDtypeStruct((t, MEM_WIDTH), BF16)],
        compiler_params=_cparams("parallel"),
        name="dsa_proj",
    )(x, w_in_bf, q_norm_g.reshape(q_norm_g.shape[0], 1, Q_RANK), w_uq_bf, w_uq_idx_bf, rope_tab)


def _lane_fold(x, op, acc):
    for t in range(x.shape[1] // LANES):
        acc = op(acc, x[:, t * LANES:(t + 1) * LANES])
    return acc


def _key_to_float(key):
    return lax.bitcast_convert_type(key ^ (lax.shift_right_arithmetic(key, 31) & 0x7FFFFFFF), F32)


KEY_NEG_INF = INT_MIN + 2 ** 23 - 1


def _dsa_kernel(q_ref, qi_ref, wi_ref, k_ref, v_ref, ki_ref, o_ref, score_ref, s_ref, m_ref, l_ref, acc_ref,
                *, tiles_per_seq, n_sel):
    tq = q_ref.shape[0]
    seq = k_ref.shape[0]
    kc = min(DSA_KC, seq)
    jt = pl.program_id(0) % tiles_per_seq
    n_chunks = (jt * tq + tq + kc - 1) // kc
    qpos = jt * tq + lax.broadcasted_iota(I32, (tq, 1), 0)

    def chunk_start(c):
        return pl.multiple_of(c * kc, kc)

    def key_positions(c):
        return c * kc + lax.broadcasted_iota(I32, (1, kc), 1)

    wi = wi_ref[...][:, IDX_DIM:IDX_DIM + IDX_HEADS] * (IDX_HEADS ** -0.5)
    qi = qi_ref[...]
    qi_heads = [qi[:, h * IDX_DIM:(h + 1) * IDX_DIM] for h in range(IDX_HEADS)]
    wi_heads = [wi[:, h:h + 1] for h in range(IDX_HEADS)]

    def score_chunk(c, carry):
        off = chunk_start(c)
        kch = ki_ref[pl.ds(off, kc), 0:IDX_DIM]
        score = jnp.zeros((tq, kc), F32)
        for h in range(IDX_HEADS):
            score = score + wi_heads[h] * jnp.maximum(_dot_nt(qi_heads[h], kch), 0.0)
        score_ref[:, pl.ds(off, kc)] = jnp.where(key_positions(c) <= qpos, score, -jnp.inf)
        return carry

    lax.fori_loop(0, n_chunks, score_chunk, 0)

    def count(pred):
        def body(c, acc):
            sc = score_ref[:, pl.ds(chunk_start(c), kc)]
            return _lane_fold(jnp.where(pred(sc, c), 1.0, 0.0), jnp.add, acc)
        acc = lax.fori_loop(0, n_chunks, body, jnp.zeros((tq, LANES), F32))
        return jnp.sum(acc, axis=1, keepdims=True)

    def value_bit(it, lo):
        cand = lo + lax.shift_left(jnp.int32(1), 31 - it)
        cand_f = _key_to_float(cand)
        cnt = count(lambda sc, c: sc >= cand_f)
        take = jnp.logical_or(cnt >= n_sel, cand < KEY_NEG_INF)
        return jnp.where(take, cand, lo)

    lo = lax.fori_loop(0, 32, value_bit, jnp.full((tq, 1), INT_MIN, I32))
    thr = _key_to_float(jnp.maximum(lo, KEY_NEG_INF))
    need = n_sel - count(lambda sc, c: sc > thr)
    n_eq = count(lambda sc, c: sc == thr)
    surplus = jnp.where(n_eq > need, jnp.where(lo > KEY_NEG_INF, 1, 0), 0)

    def tie_cut():
        idx_bits = max(seq - 1, 1).bit_length()

        def index_bit(it, cut):
            cand = cut + lax.shift_left(jnp.int32(1), idx_bits - 1 - it)
            cnt = count(lambda sc, c: jnp.logical_and(sc == thr, key_positions(c) < cand))
            return jnp.where(cnt < need, cand, cut)

        cut = lax.fori_loop(0, idx_bits, index_bit, jnp.zeros((tq, 1), I32))
        return jnp.where(surplus > 0, cut, seq)

    cut = lax.cond(jnp.max(surplus) > 0, tie_cut, lambda: jnp.full((tq, 1), seq, I32))

    def bias_chunk(c, carry):
        off = chunk_start(c)
        sc = score_ref[:, pl.ds(off, kc)]
        kpos = key_positions(c)
        tie = jnp.logical_and(sc == thr, kpos <= cut)
        sel = jnp.logical_and(kpos <= qpos, jnp.logical_or(sc > thr, tie))
        score_ref[:, pl.ds(off, kc)] = jnp.where(sel, 0.0, NEG_BIG)
        return carry

    lax.fori_loop(0, n_chunks, bias_chunk, 0)

    q = q_ref[...]
    for g in range(KV_HEADS):
        heads = range(g * HEADS_PER_KV, (g + 1) * HEADS_PER_KV)
        qg = jnp.concatenate([q[:, h * HEAD_DIM:(h + 1) * HEAD_DIM] for h in heads], axis=0)
        kv_cols = slice(g * HEAD_DIM, (g + 1) * HEAD_DIM)
        m_ref[...] = jnp.full(m_ref.shape, NEG_BIG, F32)
        l_ref[...] = jnp.zeros(l_ref.shape, F32)
        acc_ref[...] = jnp.zeros(acc_ref.shape, F32)

        def logits_chunk(c, carry, qg=qg, kv_cols=kv_cols):
            off = chunk_start(c)
            bias = score_ref[:, pl.ds(off, kc)]
            s = _dot_nt(qg, k_ref[pl.ds(off, kc), kv_cols]) + jnp.concatenate([bias] * HEADS_PER_KV, axis=0)
            s_ref[:, pl.ds(off, kc)] = s
            m_ref[...] = _lane_fold(s, jnp.maximum, m_ref[...])
            return carry

        lax.fori_loop(0, n_chunks, logits_chunk, 0)
        m = jnp.max(m_ref[...], axis=1, keepdims=True)

        def pv_chunk(c, carry, m=m, kv_cols=kv_cols):
            off = chunk_start(c)
            p = jnp.exp(s_ref[:, pl.ds(off, kc)] - m)
            l_ref[...] = _lane_fold(p, jnp.add, l_ref[...])
            acc_ref[...] += _dot(p.astype(BF16), v_ref[pl.ds(off, kc), kv_cols])
            return carry

        lax.fori_loop(0, n_chunks, pv_chunk, 0)
        out = acc_ref[...] / jnp.sum(l_ref[...], axis=1, keepdims=True)
        for r, h in enumerate(heads):
            o_ref[:, h * HEAD_DIM:(h + 1) * HEAD_DIM] = out[r * tq:(r + 1) * tq].astype(o_ref.dtype)


def _dsa_attention(q, qi, wi, k, v, ki, seq):
    t = q.shape[0]
    tq = DSA_TQ
    nts = seq // tq
    n_sel = min(TOPK_MAX, seq // 4)
    per_seq = lambda w: pl.BlockSpec((seq, w), lambda i: (i // nts, 0))
    return pl.pallas_call(
        functools.partial(_dsa_kernel, tiles_per_seq=nts, n_sel=n_sel),
        grid=(t // tq,),
        in_specs=[pl.BlockSpec((tq, MIX_WIDTH), lambda i: (i, 0)),
                  pl.BlockSpec((tq, IDX_HEADS * IDX_DIM), lambda i: (i, 0)),
                  pl.BlockSpec((tq, LANES), lambda i: (i, 0)),
                  per_seq(KV_WIDTH), per_seq(KV_WIDTH), per_seq(LANES)],
        out_specs=pl.BlockSpec((tq, MIX_WIDTH), lambda i: (i, 0)),
        out_shape=jax.ShapeDtypeStruct((t, MIX_WIDTH), BF16),
        scratch_shapes=[pltpu.VMEM((tq, seq), F32),
                        pltpu.VMEM((HEADS_PER_KV * tq, seq), F32),
                        pltpu.VMEM((HEADS_PER_KV * tq, LANES), F32),
                        pltpu.VMEM((HEADS_PER_KV * tq, LANES), F32),
                        pltpu.VMEM((HEADS_PER_KV * tq, HEAD_DIM), F32)],
        compiler_params=_cparams("parallel"),
        name="dsa_attention",
    )(q, qi, wi, k, v, ki)


def _mem_attn_kernel(q_ref, k_ref, v_ref, o_ref):
    q = q_ref[...]
    for h in range(MEM_HEADS):
        sl = slice(h * HEAD_DIM, (h + 1) * HEAD_DIM)
        logits = _dot_nt(q[:, sl], k_ref[:, sl])
        e = jnp.exp(logits - jnp.max(logits, axis=-1, keepdims=True))
        p = e / jnp.sum(e, axis=-1, keepdims=True)
        o_ref[:, sl] = _dot(p.astype(BF16), v_ref[:, sl]).astype(o_ref.dtype)


def _mem_attention(qm, kv_mem, seq, mem_len):
    t = qm.shape[0]
    tq = MEM_TQ
    nqs = seq // tq
    return pl.pallas_call(
        _mem_attn_kernel,
        grid=(t // tq,),
        in_specs=[pl.BlockSpec((tq, MEM_WIDTH), lambda i: (i, 0)),
                  pl.BlockSpec((mem_len, MEM_WIDTH), lambda i: (i // nqs, 0)),
                  pl.BlockSpec((mem_len, MEM_WIDTH), lambda i: (i // nqs, 1))],
        out_specs=pl.BlockSpec((tq, MEM_WIDTH), lambda i: (i, 0)),
        out_shape=jax.ShapeDtypeStruct((t, MEM_WIDTH), BF16),
        compiler_params=_cparams("parallel"),
        name="mem_attention",
    )(qm, kv_mem, kv_mem)


def _outproj_ln_kernel(x_ref, mix_ref, mo_ref, w_ref, g_ref, b_ref, o_ref, wbf_ref):
    @pl.when(pl.program_id(0) == 0)
    def _():
        wbf_ref[...] = w_ref[...].astype(BF16)

    y = _dot(mix_ref[...], wbf_ref[0:MIX_WIDTH, :]) + _dot(mo_ref[...], wbf_ref[MIX_WIDTH:, :])
    o_ref[...] = _layer_norm(ALPHA * x_ref[...] + y, g_ref[...], b_ref[...])


def _outproj_ln(x, mix, mem_out, w_out, ln_g, ln_b, layer):
    t = x.shape[0]
    tm = LN_TM
    vec = lambda: pl.BlockSpec((None, 1, D_MODEL), lambda i: (layer, 0, 0))
    return pl.pallas_call(
        _outproj_ln_kernel,
        grid=(t // tm,),
        in_specs=[pl.BlockSpec((tm, D_MODEL), lambda i: (i, 0)),
                  pl.BlockSpec((tm, MIX_WIDTH), lambda i: (i, 0)),
                  pl.BlockSpec((tm, MEM_WIDTH), lambda i: (i, 0)),
                  pl.BlockSpec((None, D_MODEL, D_MODEL), lambda i: (layer, 0, 0)),
                  vec(), vec()],
        out_specs=pl.BlockSpec((tm, D_MODEL), lambda i: (i, 0)),
        out_shape=jax.ShapeDtypeStruct((t, D_MODEL), F32),
        scratch_shapes=[pltpu.VMEM((D_MODEL, D_MODEL), BF16)],
        compiler_params=_cparams("arbitrary"),
        name="outproj_ln",
    )(x, mix, mem_out, w_out, ln_g.reshape(DEPTH, 1, D_MODEL), ln_b.reshape(DEPTH, 1, D_MODEL))


def _router_kernel(x_ref, w_ref, b_ref, idx_ref, gate_ref):
    logits = jnp.dot(x_ref[...], w_ref[...], preferred_element_type=F32,
                     precision=lax.Precision.HIGHEST) + b_ref[...]
    lane = lax.broadcasted_iota(I32, logits.shape, 1).astype(F32)
    vals, idxs = [], []
    for _ in range(TOP_K):
        m = jnp.max(logits, axis=-1, keepdims=True)
        ix = jnp.min(jnp.where(logits == m, lane, float(N_EXPERTS)), axis=-1, keepdims=True)
        vals.append(m)
        idxs.append(ix)
        logits = jnp.where(lane == ix, -jnp.inf, logits)
    e = jnp.exp(jnp.concatenate(vals, axis=1) - vals[0])
    idx_ref[...] = jnp.concatenate(idxs, axis=1).astype(I32)
    gate_ref[...] = e / jnp.sum(e, axis=-1, keepdims=True)


def _router(x, router_w, router_b, layer):
    t = x.shape[0]
    tm = ROUTER_TM
    return pl.pallas_call(
        _router_kernel,
        grid=(t // tm,),
        in_specs=[pl.BlockSpec((tm, D_MODEL), lambda i: (i, 0)),
                  pl.BlockSpec((None, D_MODEL, N_EXPERTS), lambda i: (layer, 0, 0)),
                  pl.BlockSpec((None, 1, N_EXPERTS), lambda i: (layer, 0, 0))],
        out_specs=[pl.BlockSpec((tm, TOP_K), lambda i: (i, 0)), pl.BlockSpec((tm, TOP_K), lambda i: (i, 0))],
        out_shape=[jax.ShapeDtypeStruct((t, TOP_K), I32), jax.ShapeDtypeStruct((t, TOP_K), F32)],
        compiler_params=_cparams("parallel"),
        name="router",
    )(x, router_w, router_b.reshape(DEPTH, 1, N_EXPERTS))


def _dispatch_plan(top_idx):
    t = top_idx.shape[0]
    n_assign = t * TOP_K
    tm = MOE_TM
    n_tiles = n_assign // tm + N_EXPERTS + 1
    e = top_idx.reshape(n_assign)
    experts = jnp.arange(N_EXPERTS, dtype=I32)
    onehot = (e[:, None] == experts[None, :]).astype(I32)
    csum = jnp.cumsum(onehot, axis=0)
    counts = csum[-1]
    padded = ((counts + tm - 1) // tm) * tm
    group_end = jnp.cumsum(padded)
    group_start = group_end - padded
    slot = jnp.sum((csum - onehot + group_start[None, :]) * onehot, axis=1)
    n_used = group_end[-1] // tm
    tile = jnp.arange(n_tiles, dtype=I32)
    tile_expert = jnp.minimum(jnp.sum((tile[:, None] * tm >= group_end[None, :]).astype(I32), axis=1), N_EXPERTS - 1)
    last_expert = jnp.sum(jnp.where(tile == n_used - 1, tile_expert, 0))
    tile_expert = jnp.where(tile < n_used, tile_expert, last_expert).astype(I32)
    stride = 2 * n_assign
    pad = jnp.arange(tm, dtype=I32)
    pad_keys = jnp.where(pad[None, :] < (padded - counts)[:, None],
                         experts[:, None] * stride + n_assign + pad[None, :], jnp.iinfo(I32).max)
    keys = jnp.sort(jnp.concatenate([e * stride + jnp.arange(n_assign, dtype=I32), pad_keys.reshape(-1)]))
    index = keys % stride
    row_ids = jnp.where(index < n_assign, index // TOP_K, 0)
    row_ids = jnp.concatenate([row_ids, jnp.zeros(((n_tiles - n_assign // tm - N_EXPERTS) * tm,), I32)])
    nonempty = counts > 0
    later = jnp.where(nonempty[None, :] & (experts[None, :] > experts[:, None]), experts[None, :], N_EXPERTS)
    next_expert = jnp.min(later, axis=1)
    next_expert = jnp.where(next_expert < N_EXPERTS, next_expert, -1)
    group_parity = (jnp.cumsum(nonempty.astype(I32)) - 1) % 2
    info = jnp.stack([tile_expert, next_expert[tile_expert], group_parity[tile_expert]]).astype(I32)
    return row_ids.reshape(n_tiles, 1, tm), info, n_used.reshape(1).astype(I32), slot.reshape(t, TOP_K)


def _gather_rows(src_hbm, ids_ref, n_rows, dst_ref, sem):
    def body(r, carry):
        pltpu.make_async_copy(src_hbm.at[pl.ds(ids_ref[0, 0, r], 1)], dst_ref.at[pl.ds(r, 1)], sem).start()
        return carry
    lax.fori_loop(0, n_rows, body, 0, unroll=8)


def _wait_rows(dst_ref, sem):
    pltpu.make_async_copy(dst_ref, dst_ref, sem).wait()


def _moe_kernel(info_ref, nu_ref, ids_ref, ids_1_ref, ids_2_ref, x_hbm, wgu_hbm, bgu_ref, wd_hbm, bd_ref, o_ref,
                xbuf0, xbuf1, xbuf2, sems, wgu_f32, wd_f32, wsems, wgu_bf, wd_bf, *, layer):
    j = pl.program_id(0)
    n_used = nu_ref[0]
    tm = xbuf0.shape[0]
    bufs = ((xbuf0, sems.at[0]), (xbuf1, sems.at[1]), (xbuf2, sems.at[2]))
    expert, next_expert, wslot = info_ref[0, j], info_ref[1, j], info_ref[2, j]

    def weight_copies(e, slot):
        return (pltpu.make_async_copy(wgu_hbm.at[layer, e], wgu_f32.at[slot], wsems.at[0, slot]),
                pltpu.make_async_copy(wd_hbm.at[layer, e], wd_f32.at[slot], wsems.at[1, slot]))

    @pl.when(j == 0)
    def _():
        for cp in weight_copies(expert, wslot):
            cp.start()
        _gather_rows(x_hbm, ids_ref, tm, xbuf0, sems.at[0])
        _gather_rows(x_hbm, ids_1_ref, tm, xbuf1, sems.at[1])

    def expert_tile(cur, nxt):
        _wait_rows(*cur)

        @pl.when(jnp.logical_or(j == 0, expert != info_ref[0, jnp.maximum(j - 1, 0)]))
        def _():
            for cp in weight_copies(expert, wslot):
                cp.wait()

            @pl.when(next_expert >= 0)
            def _():
                for cp in weight_copies(next_expert, 1 - wslot):
                    cp.start()

            wgu_bf[...] = wgu_f32[wslot].astype(BF16)
            wd_bf[...] = wd_f32[wslot].astype(BF16)

        for r in range(tm):
            pltpu.make_async_copy(x_hbm.at[pl.ds(ids_2_ref[0, 0, r], 1)], nxt[0].at[pl.ds(r, 1)],
                                  nxt[1]).start(priority=r % 2)
        h = _dot(cur[0][...].astype(BF16), wgu_bf[...]) + bgu_ref[...]
        gate = jnp.minimum(h[:, :D_EXPERT], SWIGLU_LIMIT)
        up = jnp.clip(h[:, D_EXPERT:], -SWIGLU_LIMIT, SWIGLU_LIMIT)
        act = (up + 1.0) * gate * jax.nn.sigmoid(SWIGLU_ALPHA * gate)
        o_ref[...] = _dot(act.astype(BF16), wd_bf[...]) + bd_ref[...]

    for m in range(3):
        @pl.when(jnp.logical_and(j < n_used, j % 3 == m))
        def _(m=m):
            expert_tile(bufs[m], bufs[(m + 2) % 3])

        @pl.when(jnp.logical_and(jnp.logical_and(j >= n_used, j <= n_used + 1), j % 3 == m))
        def _(m=m):
            _wait_rows(*bufs[m])

    @pl.when(j >= n_used)
    def _():
        o_ref[...] = jnp.zeros_like(o_ref)


def _moe_experts(x, row_ids, tile_info, n_used, w_gate_up, b_gate_up, w_down, b_down, layer):
    n_tiles = row_ids.shape[0]
    tm = MOE_TM
    grid_spec = pltpu.PrefetchScalarGridSpec(
        num_scalar_prefetch=2,
        grid=(n_tiles,),
        in_specs=[pl.BlockSpec((1, 1, tm), lambda j, info, nu: (j, 0, 0), memory_space=pltpu.SMEM),
                  pl.BlockSpec((1, 1, tm), lambda j, info, nu: (jnp.minimum(j + 1, n_tiles - 1), 0, 0),
                               memory_space=pltpu.SMEM),
                  pl.BlockSpec((1, 1, tm), lambda j, info, nu: (jnp.minimum(j + 2, n_tiles - 1), 0, 0),
                               memory_space=pltpu.SMEM),
                  pl.BlockSpec(memory_space=pl.ANY),
                  pl.BlockSpec(memory_space=pl.ANY),
                  pl.BlockSpec((None, None, 1, 2 * D_EXPERT), lambda j, info, nu: (layer, info[0, j], 0, 0)),
                  pl.BlockSpec(memory_space=pl.ANY),
                  pl.BlockSpec((None, None, 1, D_MODEL), lambda j, info, nu: (layer, info[0, j], 0, 0))],
        out_specs=pl.BlockSpec((tm, D_MODEL), lambda j, info, nu: (j, 0)),
        scratch_shapes=[pltpu.VMEM((tm, D_MODEL), F32), pltpu.VMEM((tm, D_MODEL), F32), pltpu.VMEM((tm, D_MODEL), F32),
                        pltpu.SemaphoreType.DMA((3,)),
                        pltpu.VMEM((2, D_MODEL, 2 * D_EXPERT), F32), pltpu.VMEM((2, D_EXPERT, D_MODEL), F32),
                        pltpu.SemaphoreType.DMA((2, 2)),
                        pltpu.VMEM((D_MODEL, 2 * D_EXPERT), BF16), pltpu.VMEM((D_EXPERT, D_MODEL), BF16)],
    )
    return pl.pallas_call(
        functools.partial(_moe_kernel, layer=layer),
        grid_spec=grid_spec,
        out_shape=jax.ShapeDtypeStruct((n_tiles * tm, D_MODEL), F32),
        compiler_params=_cparams("arbitrary"),
        name="moe_experts",
    )(tile_info, n_used, row_ids, row_ids, row_ids, x, w_gate_up,
      b_gate_up.reshape(DEPTH, N_EXPERTS, 1, 2 * D_EXPERT), w_down, b_down.reshape(DEPTH, N_EXPERTS, 1, D_MODEL))


def _combine_ln_kernel(slot_ref, slot_1_ref, slot_2_ref, ys_hbm, gate_ref, x_ref, g_ref, b_ref, o_ref,
                       buf0, buf1, buf2, sems, *, n_steps):
    i = pl.program_id(0)
    rows = buf0.shape[0]
    tc = rows // TOP_K
    bufs = ((buf0, sems.at[0]), (buf1, sems.at[1]), (buf2, sems.at[2]))

    @pl.when(i == 0)
    def _():
        _gather_rows(ys_hbm, slot_ref, rows, buf0, sems.at[0])
        _gather_rows(ys_hbm, slot_1_ref, rows, buf1, sems.at[1])

    def token_tile(cur, nxt):
        _wait_rows(*cur)
        if nxt is not None:
            for r in range(rows):
                pltpu.make_async_copy(ys_hbm.at[pl.ds(slot_2_ref[0, 0, r], 1)], nxt[0].at[pl.ds(r, 1)],
                                      nxt[1]).start(priority=r % 2)
        gates = gate_ref[...]
        f = gates[:, 0:1] * cur[0][0:tc, :]
        for k in range(1, TOP_K):
            f = f + gates[:, k:k + 1] * cur[0][k * tc:(k + 1) * tc, :]
        o_ref[...] = _layer_norm(ALPHA * x_ref[...] + f, g_ref[...], b_ref[...])

    for m in range(3):
        @pl.when(jnp.logical_and(i < n_steps - 2, i % 3 == m))
        def _(m=m):
            token_tile(bufs[m], bufs[(m + 2) % 3])

    for last in (n_steps - 2, n_steps - 1):
        @pl.when(i == last)
        def _(last=last):
            token_tile(bufs[last % 3], None)


def _combine_ln(x, ys, slots, gates, ln_g, ln_b, layer):
    t = x.shape[0]
    tc = COMB_TC
    n = t // tc
    slots_km = slots.reshape(n, tc, TOP_K).transpose(0, 2, 1).reshape(n, 1, TOP_K * tc)
    vec = lambda: pl.BlockSpec((None, 1, D_MODEL), lambda i: (layer, 0, 0))
    return pl.pallas_call(
        functools.partial(_combine_ln_kernel, n_steps=n),
        grid=(n,),
        in_specs=[pl.BlockSpec((1, 1, TOP_K * tc), lambda i: (i, 0, 0), memory_space=pltpu.SMEM),
                  pl.BlockSpec((1, 1, TOP_K * tc), lambda i: (jnp.minimum(i + 1, n - 1), 0, 0),
                               memory_space=pltpu.SMEM),
                  pl.BlockSpec((1, 1, TOP_K * tc), lambda i: (jnp.minimum(i + 2, n - 1), 0, 0),
                               memory_space=pltpu.SMEM),
                  pl.BlockSpec(memory_space=pl.ANY),
                  pl.BlockSpec((tc, TOP_K), lambda i: (i, 0)),
                  pl.BlockSpec((tc, D_MODEL), lambda i: (i, 0)),
                  vec(), vec()],
        out_specs=pl.BlockSpec((tc, D_MODEL), lambda i: (i, 0)),
        out_shape=jax.ShapeDtypeStruct((t, D_MODEL), F32),
        scratch_shapes=[pltpu.VMEM((TOP_K * tc, D_MODEL), F32), pltpu.VMEM((TOP_K * tc, D_MODEL), F32),
                        pltpu.VMEM((TOP_K * tc, D_MODEL), F32), pltpu.SemaphoreType.DMA((3,))],
        compiler_params=_cparams("arbitrary"),
        name="combine_ln",
    )(slots_km, slots_km, slots_km, ys, gates, x, ln_g.reshape(DEPTH, 1, D_MODEL), ln_b.reshape(DEPTH, 1, D_MODEL))


def _rope_tables(seq):
    inv = ROPE_THETA ** (-jnp.arange(0, ROT_DIM, 2, dtype=F32) / ROT_DIM)
    ang = jnp.arange(seq, dtype=F32)[:, None] * inv[None, :]
    cos, sin = jnp.cos(ang), jnp.sin(ang)
    half = ROT_DIM // 2
    rest = HEAD_DIM - ROT_DIM
    zeros_h = jnp.zeros((seq, half), F32)
    head = lambda a, b, fill: jnp.concatenate([a, b, jnp.full((seq, rest), fill, F32)], axis=1)
    tab = jnp.stack([head(cos, cos, 1.0), head(zeros_h, sin, 0.0), head(-sin, zeros_h, 0.0)])
    return jnp.tile(tab, (1, 1, LANES // HEAD_DIM))


def kernel(x, mem, a_w_in, a_conv_w, b_w_in, b_q_norm_g, b_w_uq, b_w_uq_idx, c_w_in, c_sinks, w_out, w_kv_mem,
           ln1_g, ln1_b, router_w, router_b, w_gate_up, b_gate_up, w_down, b_down, ln2_g, ln2_b):
    batch, seq, _ = x.shape
    mem_len = mem.shape[1]
    t = batch * seq
    xs = x.reshape(t, D_MODEL)
    mem2 = mem.reshape(batch * mem_len, D_MODEL)
    rope_tab = _rope_tables(seq)

    o1 = Q_RANK
    o3 = o1 + 2 * KV_WIDTH
    o5 = o3 + IDX_DIM + IDX_HEADS
    b_w = jnp.concatenate([b_w_in[:, :, :o3], b_w_in[:, :, o5:], b_w_in[:, :, o3:o5],
                           jnp.zeros(b_w_in.shape[:2] + (LANES - IDX_DIM - IDX_HEADS,), b_w_in.dtype)], axis=2)
    a_w, b_w, c_w = a_w_in.astype(BF16), b_w.astype(BF16), c_w_in.astype(BF16)
    b_wq, b_wqi = b_w_uq.astype(BF16), b_w_uq_idx.astype(BF16)

    for i in range(DEPTH):
        kind, j = i % N_MIXERS, i // N_MIXERS
        if kind == 0:
            mix, qm = _conv_mixer(xs, a_w, a_conv_w, j, seq)
        elif kind == 1:
            q, qi, k, v, ki, wi, qm = _dsa_proj(xs, b_w, b_q_norm_g, b_wq, b_wqi, j, rope_tab, seq)
            mix = _dsa_attention(q, qi, wi, k, v, ki, seq)
        else:
            q, k, v, qm = _swa_proj(xs, c_w, j, rope_tab, seq)
            mix = _swa_mixer(q, k, v, c_sinks, j, seq)
        kv_mem = _matmul(mem2, w_kv_mem, i, BF16, min(MM_TM, batch * mem_len), MM_TN)
        mem_out = _mem_attention(qm, kv_mem, seq, mem_len)
        xs = _outproj_ln(xs, mix, mem_out, w_out, ln1_g, ln1_b, i)
        top_idx, gates = _router(xs, router_w, router_b, i)
        row_ids, tile_info, n_used, slots = _dispatch_plan(top_idx)
        ys = _moe_experts(xs, row_ids, tile_info, n_used, w_gate_up, b_gate_up, w_down, b_down, i)
        xs = _combine_ln(xs, ys, slots, gates, ln2_g, ln2_b, i)
    return xs.reshape(batch, seq, D_MODEL)
```

```python
import functools

import jax
import jax.numpy as jnp
from jax import lax
from jax.experimental import pallas as pl
from jax.experimental.pallas import tpu as pltpu

F32 = jnp.float32
BF16 = jnp.bfloat16
I32 = jnp.int32

D_MODEL = 1024
DEPTH = 4
N_MIXERS = 3
HEAD_DIM = 64
MIX_HEADS = 12
MIX_WIDTH = MIX_HEADS * HEAD_DIM
MEM_HEADS = 4
MEM_WIDTH = MEM_HEADS * HEAD_DIM
KV_HEADS = 4
KV_WIDTH = KV_HEADS * HEAD_DIM
HEADS_PER_KV = MIX_HEADS // KV_HEADS
ROT_DIM = HEAD_DIM // 4
ROPE_THETA = 500000.0
CONV_WIDTH = 3
Q_RANK = 256
IDX_HEADS = 8
IDX_DIM = 64
TOPK_MAX = 256
BLOCK = 128
N_EXPERTS = 32
TOP_K = 4
D_EXPERT = 1024
SWIGLU_LIMIT = 7.0
SWIGLU_ALPHA = 1.702
ALPHA = (2 * DEPTH) ** 0.25
LN_EPS = 1e-5

LANES = 128
QK_SCALE = HEAD_DIM ** -0.5
INT_MIN = -(2 ** 31)
NEG_BIG = -1e30
VMEM_LIMIT = 48 * 1024 * 1024

MM_TM, MM_TN = 1024, 512
PROJ_TM = 512
CONV_TS = 512
MEM_TQ = 512
LN_TM = 512
ROUTER_TM = 512
MOE_TM = 256
COMB_TC = 128
DSA_TQ = 256
DSA_KC = 512
PREP_TM = 256


def _cparams(*sem):
    return pltpu.CompilerParams(dimension_semantics=sem, vmem_limit_bytes=VMEM_LIMIT)


def _dot(a, b):
    return jnp.dot(a, b, preferred_element_type=F32)


def _dot_nt(a, b):
    return lax.dot_general(a, b, (((1,), (1,)), ((), ())), preferred_element_type=F32)


def _layer_norm(h, g, b):
    mu = jnp.mean(h, axis=-1, keepdims=True)
    d = h - mu
    var = jnp.mean(d * d, axis=-1, keepdims=True)
    return d * lax.rsqrt(var + LN_EPS) * g + b


def _rope(x, c, s1, s2):
    outs = []
    for k in range(x.shape[1] // LANES):
        xs = x[:, k * LANES:(k + 1) * LANES]
        outs.append(xs * c + pltpu.roll(xs, 8, 1) * s1 + pltpu.roll(xs, LANES - 8, 1) * s2)
    return outs[0] if len(outs) == 1 else jnp.concatenate(outs, axis=1)


def _matmul_kernel(x_ref, w_ref, o_ref):
    o_ref[...] = _dot(x_ref[...].astype(BF16), w_ref[...].astype(BF16)).astype(o_ref.dtype)


def _matmul(x, w, layer, out_dtype, tm, tn):
    m, k = x.shape
    n = w.shape[-1]
    return pl.pallas_call(
        _matmul_kernel,
        grid=(m // tm, n // tn),
        in_specs=[pl.BlockSpec((tm, k), lambda i, j: (i, 0)),
                  pl.BlockSpec((None, k, tn), lambda i, j: (layer, 0, j))],
        out_specs=pl.BlockSpec((tm, tn), lambda i, j: (i, j)),
        out_shape=jax.ShapeDtypeStruct((m, n), out_dtype),
        compiler_params=_cparams("parallel", "parallel"),
        name="proj_matmul",
    )(x, w)


def _conv_kernel(x_ref, w_ref, cw_ref, mix_ref, qm_ref, zz_ref, *, tiles_per_seq):
    ts = x_ref.shape[0]
    wb = MIX_WIDTH

    @pl.when(pl.program_id(0) % tiles_per_seq == 0)
    def _():
        zz_ref[0:8, :] = jnp.zeros((8, wb), F32)

    proj = _dot(x_ref[...].astype(BF16), w_ref[...])
    z = proj[:, 2 * wb:3 * wb] * proj[:, 0:wb]
    zz_ref[8:, :] = z
    cw = cw_ref[...]
    conv = cw[0:1, :] * zz_ref[6:6 + ts, :] + cw[1:2, :] * zz_ref[7:7 + ts, :] + cw[2:3, :] * z
    mix_ref[...] = (proj[:, wb:2 * wb] * conv).astype(mix_ref.dtype)
    qm_ref[...] = (proj[:, 3 * wb:] * QK_SCALE).astype(qm_ref.dtype)
    zz_ref[0:8, :] = zz_ref[ts:ts + 8, :]


def _conv_mixer(x, w_in_bf, conv_w, layer, seq):
    t = x.shape[0]
    ts = CONV_TS
    n_in = w_in_bf.shape[-1]
    return pl.pallas_call(
        functools.partial(_conv_kernel, tiles_per_seq=seq // ts),
        grid=(t // ts,),
        in_specs=[pl.BlockSpec((ts, D_MODEL), lambda i: (i, 0)),
                  pl.BlockSpec((None, D_MODEL, n_in), lambda i: (layer, 0, 0)),
                  pl.BlockSpec((None, CONV_WIDTH, MIX_WIDTH), lambda i: (layer, 0, 0))],
        out_specs=[pl.BlockSpec((ts, MIX_WIDTH), lambda i: (i, 0)), pl.BlockSpec((ts, MEM_WIDTH), lambda i: (i, 0))],
        out_shape=[jax.ShapeDtypeStruct((t, MIX_WIDTH), BF16), jax.ShapeDtypeStruct((t, MEM_WIDTH), BF16)],
        scratch_shapes=[pltpu.VMEM((ts + 8, MIX_WIDTH), F32)],
        compiler_params=_cparams("arbitrary"),
        name="conv_mixer",
    )(x, w_in_bf, conv_w)


def _swa_proj_kernel(x_ref, w_ref, tab_ref, q_out, k_out, v_out, qm_out):
    tab = tab_ref[...]
    c, s1, s2 = tab[0], tab[1], tab[2]
    proj = _dot(x_ref[...].astype(BF16), w_ref[...])
    o1, o2, o3 = MIX_WIDTH, MIX_WIDTH + KV_WIDTH, MIX_WIDTH + 2 * KV_WIDTH
    q_out[...] = (_rope(proj[:, :o1], c, s1, s2) * QK_SCALE).astype(BF16)
    k_out[...] = _rope(proj[:, o1:o2], c, s1, s2).astype(BF16)
    v_out[...] = proj[:, o2:o3].astype(BF16)
    qm_out[...] = (proj[:, o3:] * QK_SCALE).astype(BF16)


def _swa_proj(x, w_in_bf, layer, rope_tab, seq):
    t = x.shape[0]
    tm = PROJ_TM
    nts = seq // tm
    n_in = w_in_bf.shape[-1]
    row = lambda w: pl.BlockSpec((tm, w), lambda i: (i, 0))
    return pl.pallas_call(
        _swa_proj_kernel,
        grid=(t // tm,),
        in_specs=[row(D_MODEL),
                  pl.BlockSpec((None, D_MODEL, n_in), lambda i: (layer, 0, 0)),
                  pl.BlockSpec((3, tm, LANES), lambda i: (0, i % nts, 0))],
        out_specs=[row(MIX_WIDTH), row(KV_WIDTH), row(KV_WIDTH), row(MEM_WIDTH)],
        out_shape=[jax.ShapeDtypeStruct((t, MIX_WIDTH), BF16), jax.ShapeDtypeStruct((t, KV_WIDTH), BF16),
                   jax.ShapeDtypeStruct((t, KV_WIDTH), BF16), jax.ShapeDtypeStruct((t, MEM_WIDTH), BF16)],
        compiler_params=_cparams("parallel"),
        name="swa_proj",
    )(x, w_in_bf, rope_tab)


def _swa_kernel(q_ref, kc_ref, kp_ref, vc_ref, vp_ref, sink_ref, o_ref, *, blocks_per_seq):
    first = (pl.program_id(0) % blocks_per_seq) == 0
    q = q_ref[...]
    k = jnp.concatenate([kp_ref[...], kc_ref[...]], axis=0)
    v = jnp.concatenate([vp_ref[...], vc_ref[...]], axis=0)
    qi = lax.broadcasted_iota(I32, (BLOCK, 2 * BLOCK), 0)
    r = lax.broadcasted_iota(I32, (BLOCK, 2 * BLOCK), 1)
    keep = jnp.where(r > qi, jnp.where(r <= qi + BLOCK, 1, 0), 0)
    keep = jnp.where(first, jnp.where(r >= BLOCK, keep, 0), keep)
    bias = jnp.where(keep > 0, 0.0, -jnp.inf)
    bias = jnp.concatenate([bias] * HEADS_PER_KV, axis=0)
    sinks = sink_ref[...]
    for g in range(KV_HEADS):
        heads = range(g * HEADS_PER_KV, (g + 1) * HEADS_PER_KV)
        kv_cols = slice(g * HEAD_DIM, (g + 1) * HEAD_DIM)
        qg = jnp.concatenate([q[:, h * HEAD_DIM:(h + 1) * HEAD_DIM] for h in heads], axis=0)
        sink = jnp.concatenate([jnp.broadcast_to(sinks[0:1, h:h + 1], (BLOCK, 1)) for h in heads], axis=0)
        logits = _dot_nt(qg, k[:, kv_cols]) + bias
        m = jnp.maximum(jnp.max(logits, axis=-1, keepdims=True), sink)
        e = jnp.exp(logits - m)
        p = e / (jnp.sum(e, axis=-1, keepdims=True) + jnp.exp(sink - m))
        out = _dot(p.astype(BF16), v[:, kv_cols])
        for n, h in enumerate(heads):
            o_ref[:, h * HEAD_DIM:(h + 1) * HEAD_DIM] = out[n * BLOCK:(n + 1) * BLOCK].astype(o_ref.dtype)


def _swa_mixer(q, k, v, sinks, layer, seq):
    t = q.shape[0]
    nbs = seq // BLOCK
    prev_blk = lambda i: jnp.maximum(i - 1, 0)
    return pl.pallas_call(
        functools.partial(_swa_kernel, blocks_per_seq=nbs),
        grid=(t // BLOCK,),
        in_specs=[pl.BlockSpec((BLOCK, MIX_WIDTH), lambda i: (i, 0)),
                  pl.BlockSpec((BLOCK, KV_WIDTH), lambda i: (i, 0)),
                  pl.BlockSpec((BLOCK, KV_WIDTH), lambda i: (prev_blk(i), 0)),
                  pl.BlockSpec((BLOCK, KV_WIDTH), lambda i: (i, 0)),
                  pl.BlockSpec((BLOCK, KV_WIDTH), lambda i: (prev_blk(i), 0)),
                  pl.BlockSpec((None, 1, MIX_HEADS), lambda i: (layer, 0, 0))],
        out_specs=pl.BlockSpec((BLOCK, MIX_WIDTH), lambda i: (i, 0)),
        out_shape=jax.ShapeDtypeStruct((t, MIX_WIDTH), BF16),
        compiler_params=_cparams("parallel"),
        name="swa_mixer",
    )(q, k, k, v, v, sinks.reshape(sinks.shape[0], 1, MIX_HEADS))


def _dsa_proj_kernel(x_ref, w_ref, g_ref, wq_ref, wqi_ref, tab_ref,
                     q_out, qi_out, k_out, v_out, ki_out, wi_out, qm_out):
    tab = tab_ref[...]
    c, s1, s2 = tab[0], tab[1], tab[2]
    proj = _dot(x_ref[...].astype(BF16), w_ref[...])
    o1, o2, o3, o4 = Q_RANK, Q_RANK + KV_WIDTH, Q_RANK + 2 * KV_WIDTH, Q_RANK + 2 * KV_WIDTH + MEM_WIDTH
    cq = proj[:, :o1]
    cqn = cq * lax.rsqrt(jnp.mean(cq * cq, axis=-1, keepdims=True) + LN_EPS) * g_ref[...]
    cqn = cqn.astype(BF16)
    q_out[...] = (_rope(_dot(cqn, wq_ref[...]), c, s1, s2) * QK_SCALE).astype(BF16)
    qi_out[...] = (_rope(_dot(cqn, wqi_ref[...]), c, s1, s2) * (IDX_DIM ** -0.5)).astype(BF16)
    k_out[...] = _rope(proj[:, o1:o2], c, s1, s2).astype(BF16)
    v_out[...] = proj[:, o2:o3].astype(BF16)
    qm_out[...] = (proj[:, o3:o4] * QK_SCALE).astype(BF16)
    kiw = proj[:, o4:]
    ki_out[...] = _rope(kiw, c, s1, s2).astype(BF16)
    wi_out[...] = kiw


def _dsa_proj(x, w_in_bf, q_norm_g, w_uq_bf, w_uq_idx_bf, layer, rope_tab, seq):
    t = x.shape[0]
    tm = PROJ_TM
    nts = seq // tm
    qi_w = IDX_HEADS * IDX_DIM
    n_in = w_in_bf.shape[-1]
    row = lambda w: pl.BlockSpec((tm, w), lambda i: (i, 0))
    return pl.pallas_call(
        _dsa_proj_kernel,
        grid=(t // tm,),
        in_specs=[row(D_MODEL),
                  pl.BlockSpec((None, D_MODEL, n_in), lambda i: (layer, 0, 0)),
                  pl.BlockSpec((None, 1, Q_RANK), lambda i: (layer, 0, 0)),
                  pl.BlockSpec((None, Q_RANK, MIX_WIDTH), lambda i: (layer, 0, 0)),
                  pl.BlockSpec((None, Q_RANK, qi_w), lambda i: (layer, 0, 0)),
                  pl.BlockSpec((3, tm, LANES), lambda i: (0, i % nts, 0))],
        out_specs=[row(MIX_WIDTH), row(qi_w), row(KV_WIDTH), row(KV_WIDTH), row(LANES), row(LANES), row(MEM_WIDTH)],
        out_shape=[jax.ShapeDtypeStruct((t, MIX_WIDTH), BF16), jax.ShapeDtypeStruct((t, qi_w), BF16),
                   jax.ShapeDtypeStruct((t, KV_WIDTH), BF16), jax.ShapeDtypeStruct((t, KV_WIDTH), BF16),
                   jax.ShapeDtypeStruct((t, LANES), BF16), jax.ShapeDtypeStruct((t, LANES), F32),
                   jax.ShapeDtypeStruct((t, MEM_WIDTH), BF16)],
        compiler_params=_cparams("parallel"),
        name="dsa_proj",
    )(x, w_in_bf, q_norm_g.reshape(q_norm_g.shape[0], 1, Q_RANK), w_uq_bf, w_uq_idx_bf, rope_tab)


def _lane_fold(x, op, acc):
    for t in range(x.shape[1] // LANES):
        acc = op(acc, x[:, t * LANES:(t + 1) * LANES])
    return acc


def _key_to_float(key):
    return lax.bitcast_convert_type(key ^ (lax.shift_right_arithmetic(key, 31) & 0x7FFFFFFF), F32)


KEY_NEG_INF = INT_MIN + 2 ** 23 - 1


def _dsa_kernel(q_ref, qi_ref, wi_ref, k_ref, v_ref, ki_ref, o_ref, score_ref, score_t_ref, s_ref, m_ref, l_ref,
                acc_ref, *, tiles_per_seq, n_sel):
    tq = q_ref.shape[0]
    seq = k_ref.shape[0]
    kc = min(DSA_KC, seq)
    jt = pl.program_id(0) % tiles_per_seq
    n_chunks = (jt * tq + tq + kc - 1) // kc
    qpos = jt * tq + lax.broadcasted_iota(I32, (tq, 1), 0)

    def chunk_start(c):
        return pl.multiple_of(c * kc, kc)

    def key_positions(c):
        return c * kc + lax.broadcasted_iota(I32, (1, kc), 1)

    wi = wi_ref[...][:, IDX_DIM:IDX_DIM + IDX_HEADS] * (IDX_HEADS ** -0.5)
    qi = qi_ref[...]
    qi_heads = [qi[:, h * IDX_DIM:(h + 1) * IDX_DIM] for h in range(IDX_HEADS)]
    wi_heads = [wi[:, h:h + 1] for h in range(IDX_HEADS)]

    def score_chunk(c, carry):
        off = chunk_start(c)
        kch = ki_ref[pl.ds(off, kc), 0:IDX_DIM]
        score = jnp.zeros((tq, kc), F32)
        for h in range(IDX_HEADS):
            score = score + wi_heads[h] * jnp.maximum(_dot_nt(qi_heads[h], kch), 0.0)
        score = jnp.where(key_positions(c) <= qpos, score, -jnp.inf)
        score_ref[:, pl.ds(off, kc)] = score
        score_t_ref[pl.ds(off, kc), :] = score.T
        return carry

    lax.fori_loop(0, n_chunks, score_chunk, 0)

    def key_rows(c):
        return c * kc + lax.broadcasted_iota(I32, (kc, 1), 0)

    def count(pred):
        def body(c, acc):
            ones = jnp.where(pred(score_t_ref[pl.ds(chunk_start(c), kc), :], c), 1.0, 0.0)
            parts = [acc, None, None, None]
            for r in range(kc // 8):
                row = ones[r * 8:(r + 1) * 8, :]
                parts[r % 4] = row if parts[r % 4] is None else parts[r % 4] + row
            return (parts[0] + parts[1]) + (parts[2] + parts[3])
        acc = lax.fori_loop(0, n_chunks, body, jnp.zeros((8, tq), F32))
        return jnp.sum(acc, axis=0, keepdims=True)

    def value_bit(it, lo):
        cand = lo + lax.shift_left(jnp.int32(1), 31 - it)
        cand_f = _key_to_float(cand)
        cnt = count(lambda sc, c: sc >= cand_f)
        take = jnp.logical_or(cnt >= n_sel, cand < KEY_NEG_INF)
        return jnp.where(take, cand, lo)

    lo = lax.fori_loop(0, 32, value_bit, jnp.full((1, tq), INT_MIN, I32))
    thr_row = _key_to_float(jnp.maximum(lo, KEY_NEG_INF))
    need = n_sel - count(lambda sc, c: sc > thr_row)
    n_eq = count(lambda sc, c: sc == thr_row)
    surplus = jnp.where(n_eq > need, jnp.where(lo > KEY_NEG_INF, 1, 0), 0)

    def tie_cut():
        idx_bits = max(seq - 1, 1).bit_length()

        def index_bit(it, cut):
            cand = cut + lax.shift_left(jnp.int32(1), idx_bits - 1 - it)
            cnt = count(lambda sc, c: jnp.logical_and(sc == thr_row, key_rows(c) < cand))
            return jnp.where(cnt < need, cand, cut)

        cut = lax.fori_loop(0, idx_bits, index_bit, jnp.zeros((1, tq), I32))
        return jnp.where(surplus > 0, cut, seq)

    cut_row = lax.cond(jnp.max(surplus) > 0, tie_cut, lambda: jnp.full((1, tq), seq, I32))

    def to_column(row):
        return jnp.broadcast_to(row, (LANES, tq)).T[:, 0:1]

    thr = to_column(thr_row)
    cut = to_column(cut_row.astype(F32)).astype(I32)

    def bias_chunk(c, carry):
        off = chunk_start(c)
        sc = score_ref[:, pl.ds(off, kc)]
        kpos = key_positions(c)
        tie = jnp.logical_and(sc == thr, kpos <= cut)
        sel = jnp.logical_and(kpos <= qpos, jnp.logical_or(sc > thr, tie))
        score_ref[:, pl.ds(off, kc)] = jnp.where(sel, 0.0, NEG_BIG)
        return carry

    lax.fori_loop(0, n_chunks, bias_chunk, 0)

    q = q_ref[...]
    for g in range(KV_HEADS):
        heads = range(g * HEADS_PER_KV, (g + 1) * HEADS_PER_KV)
        qg = jnp.concatenate([q[:, h * HEAD_DIM:(h + 1) * HEAD_DIM] for h in heads], axis=0)
        kv_cols = slice(g * HEAD_DIM, (g + 1) * HEAD_DIM)
        m_ref[...] = jnp.full(m_ref.shape, NEG_BIG, F32)
        l_ref[...] = jnp.zeros(l_ref.shape, F32)
        acc_ref[...] = jnp.zeros(acc_ref.shape, F32)

        def logits_chunk(c, carry, qg=qg, kv_cols=kv_cols):
            off = chunk_start(c)
            bias = score_ref[:, pl.ds(off, kc)]
            s = _dot_nt(qg, k_ref[pl.ds(off, kc), kv_cols]) + jnp.concatenate([bias] * HEADS_PER_KV, axis=0)
            s_ref[:, pl.ds(off, kc)] = s
            m_ref[...] = _lane_fold(s, jnp.maximum, m_ref[...])
            return carry

        lax.fori_loop(0, n_chunks, logits_chunk, 0)
        m = jnp.max(m_ref[...], axis=1, keepdims=True)

        def pv_chunk(c, carry, m=m, kv_cols=kv_cols):
            off = chunk_start(c)
            p = jnp.exp(s_ref[:, pl.ds(off, kc)] - m)
            l_ref[...] = _lane_fold(p, jnp.add, l_ref[...])
            acc_ref[...] += _dot(p.astype(BF16), v_ref[pl.ds(off, kc), kv_cols])
            return carry

        lax.fori_loop(0, n_chunks, pv_chunk, 0)
        out = acc_ref[...] / jnp.sum(l_ref[...], axis=1, keepdims=True)
        for r, h in enumerate(heads):
            o_ref[:, h * HEAD_DIM:(h + 1) * HEAD_DIM] = out[r * tq:(r + 1) * tq].astype(o_ref.dtype)


def _dsa_attention(q, qi, wi, k, v, ki, seq):
    t = q.shape[0]
    tq = DSA_TQ
    nts = seq // tq
    n_sel = min(TOPK_MAX, seq // 4)
    per_seq = lambda w: pl.BlockSpec((seq, w), lambda i: (i // nts, 0))
    return pl.pallas_call(
        functools.partial(_dsa_kernel, tiles_per_seq=nts, n_sel=n_sel),
        grid=(t // tq,),
        in_specs=[pl.BlockSpec((tq, MIX_WIDTH), lambda i: (i, 0)),
                  pl.BlockSpec((tq, IDX_HEADS * IDX_DIM), lambda i: (i, 0)),
                  pl.BlockSpec((tq, LANES), lambda i: (i, 0)),
                  per_seq(KV_WIDTH), per_seq(KV_WIDTH), per_seq(LANES)],
        out_specs=pl.BlockSpec((tq, MIX_WIDTH), lambda i: (i, 0)),
        out_shape=jax.ShapeDtypeStruct((t, MIX_WIDTH), BF16),
        scratch_shapes=[pltpu.VMEM((tq, seq), F32), pltpu.VMEM((seq, tq), F32),
                        pltpu.VMEM((HEADS_PER_KV * tq, seq), F32),
                        pltpu.VMEM((HEADS_PER_KV * tq, LANES), F32),
                        pltpu.VMEM((HEADS_PER_KV * tq, LANES), F32),
                        pltpu.VMEM((HEADS_PER_KV * tq, HEAD_DIM), F32)],
        compiler_params=_cparams("parallel"),
        name="dsa_attention",
    )(q, qi, wi, k, v, ki)


def _mem_attn_kernel(q_ref, k_ref, v_ref, o_ref):
    q = q_ref[...]
    for h in range(MEM_HEADS):
        sl = slice(h * HEAD_DIM, (h + 1) * HEAD_DIM)
        logits = _dot_nt(q[:, sl], k_ref[:, sl])
        e = jnp.exp(logits - jnp.max(logits, axis=-1, keepdims=True))
        p = e / jnp.sum(e, axis=-1, keepdims=True)
        o_ref[:, sl] = _dot(p.astype(BF16), v_ref[:, sl]).astype(o_ref.dtype)


def _mem_attention(qm, kv_mem, seq, mem_len):
    t = qm.shape[0]
    tq = MEM_TQ
    nqs = seq // tq
    return pl.pallas_call(
        _mem_attn_kernel,
        grid=(t // tq,),
        in_specs=[pl.BlockSpec((tq, MEM_WIDTH), lambda i: (i, 0)),
                  pl.BlockSpec((mem_len, MEM_WIDTH), lambda i: (i // nqs, 0)),
                  pl.BlockSpec((mem_len, MEM_WIDTH), lambda i: (i // nqs, 1))],
        out_specs=pl.BlockSpec((tq, MEM_WIDTH), lambda i: (i, 0)),
        out_shape=jax.ShapeDtypeStruct((t, MEM_WIDTH), BF16),
        compiler_params=_cparams("parallel"),
        name="mem_attention",
    )(qm, kv_mem, kv_mem)


def _outproj_ln_kernel(x_ref, mix_ref, mo_ref, w_ref, g_ref, b_ref, o_ref, wbf_ref):
    @pl.when(pl.program_id(0) == 0)
    def _():
        wbf_ref[...] = w_ref[...].astype(BF16)

    y = _dot(mix_ref[...], wbf_ref[0:MIX_WIDTH, :]) + _dot(mo_ref[...], wbf_ref[MIX_WIDTH:, :])
    o_ref[...] = _layer_norm(ALPHA * x_ref[...] + y, g_ref[...], b_ref[...])


def _outproj_ln(x, mix, mem_out, w_out, ln_g, ln_b, layer):
    t = x.shape[0]
    tm = LN_TM
    vec = lambda: pl.BlockSpec((None, 1, D_MODEL), lambda i: (layer, 0, 0))
    return pl.pallas_call(
        _outproj_ln_kernel,
        grid=(t // tm,),
        in_specs=[pl.BlockSpec((tm, D_MODEL), lambda i: (i, 0)),
                  pl.BlockSpec((tm, MIX_WIDTH), lambda i: (i, 0)),
                  pl.BlockSpec((tm, MEM_WIDTH), lambda i: (i, 0)),
                  pl.BlockSpec((None, D_MODEL, D_MODEL), lambda i: (layer, 0, 0)),
                  vec(), vec()],
        out_specs=pl.BlockSpec((tm, D_MODEL), lambda i: (i, 0)),
        out_shape=jax.ShapeDtypeStruct((t, D_MODEL), F32),
        scratch_shapes=[pltpu.VMEM((D_MODEL, D_MODEL), BF16)],
        compiler_params=_cparams("arbitrary"),
        name="outproj_ln",
    )(x, mix, mem_out, w_out, ln_g.reshape(DEPTH, 1, D_MODEL), ln_b.reshape(DEPTH, 1, D_MODEL))


def _router_kernel(x_ref, w_ref, b_ref, idx_ref, gate_ref):
    x = x_ref[...]
    w = w_ref[...]
    x_hi = x.astype(BF16)
    x_lo = (x - x_hi.astype(F32)).astype(BF16)
    w_hi = w.astype(BF16)
    w_lo = (w - w_hi.astype(F32)).astype(BF16)
    hi = _dot(x_hi, jnp.concatenate([w_hi, w_lo], axis=1))
    logits = (hi[:, :N_EXPERTS] + (hi[:, N_EXPERTS:] + _dot(x_lo, w_hi))) + b_ref[...]
    lane = lax.broadcasted_iota(I32, logits.shape, 1).astype(F32)
    vals, idxs = [], []
    for _ in range(TOP_K):
        m = jnp.max(logits, axis=-1, keepdims=True)
        ix = jnp.min(jnp.where(logits == m, lane, float(N_EXPERTS)), axis=-1, keepdims=True)
        vals.append(m)
        idxs.append(ix)
        logits = jnp.where(lane == ix, -jnp.inf, logits)
    e = jnp.exp(jnp.concatenate(vals, axis=1) - vals[0])
    idx_ref[...] = jnp.concatenate(idxs, axis=1).astype(I32)
    gate_ref[...] = e / jnp.sum(e, axis=-1, keepdims=True)


def _router(x, router_w, router_b, layer):
    t = x.shape[0]
    tm = ROUTER_TM
    return pl.pallas_call(
        _router_kernel,
        grid=(t // tm,),
        in_specs=[pl.BlockSpec((tm, D_MODEL), lambda i: (i, 0)),
                  pl.BlockSpec((None, D_MODEL, N_EXPERTS), lambda i: (layer, 0, 0)),
                  pl.BlockSpec((None, 1, N_EXPERTS), lambda i: (layer, 0, 0))],
        out_specs=[pl.BlockSpec((tm, TOP_K), lambda i: (i, 0)), pl.BlockSpec((tm, TOP_K), lambda i: (i, 0))],
        out_shape=[jax.ShapeDtypeStruct((t, TOP_K), I32), jax.ShapeDtypeStruct((t, TOP_K), F32)],
        compiler_params=_cparams("parallel"),
        name="router",
    )(x, router_w, router_b.reshape(DEPTH, 1, N_EXPERTS))


def _dispatch_plan(top_idx):
    t = top_idx.shape[0]
    n_assign = t * TOP_K
    tm = MOE_TM
    n_tiles = n_assign // tm + N_EXPERTS + 1
    e = top_idx.reshape(n_assign)
    experts = jnp.arange(N_EXPERTS, dtype=I32)
    onehot = (e[:, None] == experts[None, :]).astype(I32)
    csum = jnp.cumsum(onehot, axis=0)
    counts = csum[-1]
    padded = ((counts + tm - 1) // tm) * tm
    group_end = jnp.cumsum(padded)
    group_start = group_end - padded
    slot = jnp.sum((csum - onehot + group_start[None, :]) * onehot, axis=1)
    n_used = group_end[-1] // tm
    tile = jnp.arange(n_tiles, dtype=I32)
    tile_expert = jnp.minimum(jnp.sum((tile[:, None] * tm >= group_end[None, :]).astype(I32), axis=1), N_EXPERTS - 1)
    last_expert = jnp.sum(jnp.where(tile == n_used - 1, tile_expert, 0))
    tile_expert = jnp.where(tile < n_used, tile_expert, last_expert).astype(I32)
    stride = 2 * n_assign
    pad = jnp.arange(tm, dtype=I32)
    pad_keys = jnp.where(pad[None, :] < (padded - counts)[:, None],
                         experts[:, None] * stride + n_assign + pad[None, :], jnp.iinfo(I32).max)
    keys = jnp.sort(jnp.concatenate([e * stride + jnp.arange(n_assign, dtype=I32), pad_keys.reshape(-1)]))
    index = keys % stride
    row_ids = jnp.where(index < n_assign, index // TOP_K, 0)
    row_ids = jnp.concatenate([row_ids, jnp.zeros(((n_tiles - n_assign // tm - N_EXPERTS) * tm,), I32)])
    nonempty = counts > 0
    later = jnp.where(nonempty[None, :] & (experts[None, :] > experts[:, None]), experts[None, :], N_EXPERTS)
    next_expert = jnp.min(later, axis=1)
    next_expert = jnp.where(next_expert < N_EXPERTS, next_expert, -1)
    group_parity = (jnp.cumsum(nonempty.astype(I32)) - 1) % 2
    info = jnp.stack([tile_expert, next_expert[tile_expert], group_parity[tile_expert]]).astype(I32)
    return row_ids.reshape(n_tiles, 1, tm), info, n_used.reshape(1).astype(I32), slot.reshape(t, TOP_K)


def _gather_rows(src_hbm, ids_ref, n_rows, dst_ref, sem):
    def body(r, carry):
        pltpu.make_async_copy(src_hbm.at[pl.ds(ids_ref[0, 0, r], 1)], dst_ref.at[pl.ds(r, 1)], sem).start()
        return carry
    lax.fori_loop(0, n_rows, body, 0, unroll=8)


def _wait_rows(dst_ref, sem):
    pltpu.make_async_copy(dst_ref, dst_ref, sem).wait()


def _moe_kernel(info_ref, nu_ref, ids_ref, ids_1_ref, ids_2_ref, x_hbm, wgu_hbm, bgu_ref, wd_hbm, bd_ref, o_ref,
                xbuf0, xbuf1, xbuf2, sems, wgu_f32, wd_f32, wsems, wgu_bf, wd_bf, *, layer):
    j = pl.program_id(0)
    n_used = nu_ref[0]
    tm = xbuf0.shape[0]
    bufs = ((xbuf0, sems.at[0]), (xbuf1, sems.at[1]), (xbuf2, sems.at[2]))
    expert, next_expert, wslot = info_ref[0, j], info_ref[1, j], info_ref[2, j]

    def weight_copies(e, slot):
        return (pltpu.make_async_copy(wgu_hbm.at[layer, e], wgu_f32.at[slot], wsems.at[0, slot]),
                pltpu.make_async_copy(wd_hbm.at[layer, e], wd_f32.at[slot], wsems.at[1, slot]))

    @pl.when(j == 0)
    def _():
        for cp in weight_copies(expert, wslot):
            cp.start()
        _gather_rows(x_hbm, ids_ref, tm, xbuf0, sems.at[0])
        _gather_rows(x_hbm, ids_1_ref, tm, xbuf1, sems.at[1])

    def expert_tile(cur, nxt):
        _wait_rows(*cur)

        @pl.when(jnp.logical_or(j == 0, expert != info_ref[0, jnp.maximum(j - 1, 0)]))
        def _():
            for cp in weight_copies(expert, wslot):
                cp.wait()

            @pl.when(next_expert >= 0)
            def _():
                for cp in weight_copies(next_expert, 1 - wslot):
                    cp.start()

            wgu_bf[...] = wgu_f32[wslot].astype(BF16)
            wd_bf[...] = wd_f32[wslot].astype(BF16)

        for r in range(tm):
            pltpu.make_async_copy(x_hbm.at[pl.ds(ids_2_ref[0, 0, r], 1)], nxt[0].at[pl.ds(r, 1)],
                                  nxt[1]).start(priority=r % 2)
        h = _dot(cur[0][...].astype(BF16), wgu_bf[...]) + bgu_ref[...]
        gate = jnp.minimum(h[:, :D_EXPERT], SWIGLU_LIMIT)
        up = jnp.clip(h[:, D_EXPERT:], -SWIGLU_LIMIT, SWIGLU_LIMIT)
        act = (up + 1.0) * gate * jax.nn.sigmoid(SWIGLU_ALPHA * gate)
        o_ref[...] = _dot(act.astype(BF16), wd_bf[...]) + bd_ref[...]

    for m in range(3):
        @pl.when(jnp.logical_and(j < n_used, j % 3 == m))
        def _(m=m):
            expert_tile(bufs[m], bufs[(m + 2) % 3])

        @pl.when(jnp.logical_and(jnp.logical_and(j >= n_used, j <= n_used + 1), j % 3 == m))
        def _(m=m):
            _wait_rows(*bufs[m])

    @pl.when(j >= n_used)
    def _():
        o_ref[...] = jnp.zeros_like(o_ref)


def _moe_experts(x, row_ids, tile_info, n_used, w_gate_up, b_gate_up, w_down, b_down, layer):
    n_tiles = row_ids.shape[0]
    tm = MOE_TM
    grid_spec = pltpu.PrefetchScalarGridSpec(
        num_scalar_prefetch=2,
        grid=(n_tiles,),
        in_specs=[pl.BlockSpec((1, 1, tm), lambda j, info, nu: (j, 0, 0), memory_space=pltpu.SMEM),
                  pl.BlockSpec((1, 1, tm), lambda j, info, nu: (jnp.minimum(j + 1, n_tiles - 1), 0, 0),
                               memory_space=pltpu.SMEM),
                  pl.BlockSpec((1, 1, tm), lambda j, info, nu: (jnp.minimum(j + 2, n_tiles - 1), 0, 0),
                               memory_space=pltpu.SMEM),
                  pl.BlockSpec(memory_space=pl.ANY),
                  pl.BlockSpec(memory_space=pl.ANY),
                  pl.BlockSpec((None, None, 1, 2 * D_EXPERT), lambda j, info, nu: (layer, info[0, j], 0, 0)),
                  pl.BlockSpec(memory_space=pl.ANY),
                  pl.BlockSpec((None, None, 1, D_MODEL), lambda j, info, nu: (layer, info[0, j], 0, 0))],
        out_specs=pl.BlockSpec((tm, D_MODEL), lambda j, info, nu: (j, 0)),
        scratch_shapes=[pltpu.VMEM((tm, D_MODEL), F32), pltpu.VMEM((tm, D_MODEL), F32), pltpu.VMEM((tm, D_MODEL), F32),
                        pltpu.SemaphoreType.DMA((3,)),
                        pltpu.VMEM((2, D_MODEL, 2 * D_EXPERT), F32), pltpu.VMEM((2, D_EXPERT, D_MODEL), F32),
                        pltpu.SemaphoreType.DMA((2, 2)),
                        pltpu.VMEM((D_MODEL, 2 * D_EXPERT), BF16), pltpu.VMEM((D_EXPERT, D_MODEL), BF16)],
    )
    return pl.pallas_call(
        functools.partial(_moe_kernel, layer=layer),
        grid_spec=grid_spec,
        out_shape=jax.ShapeDtypeStruct((n_tiles * tm, D_MODEL), F32),
        compiler_params=_cparams("arbitrary"),
        name="moe_experts",
    )(tile_info, n_used, row_ids, row_ids, row_ids, x, w_gate_up,
      b_gate_up.reshape(DEPTH, N_EXPERTS, 1, 2 * D_EXPERT), w_down, b_down.reshape(DEPTH, N_EXPERTS, 1, D_MODEL))


def _combine_ln_kernel(slot_ref, slot_1_ref, slot_2_ref, ys_hbm, gate_ref, x_ref, g_ref, b_ref, o_ref,
                       buf0, buf1, buf2, sems, *, n_steps):
    i = pl.program_id(0)
    rows = buf0.shape[0]
    tc = rows // TOP_K
    bufs = ((buf0, sems.at[0]), (buf1, sems.at[1]), (buf2, sems.at[2]))

    @pl.when(i == 0)
    def _():
        _gather_rows(ys_hbm, slot_ref, rows, buf0, sems.at[0])
        _gather_rows(ys_hbm, slot_1_ref, rows, buf1, sems.at[1])

    def token_tile(cur, nxt):
        _wait_rows(*cur)
        if nxt is not None:
            for r in range(rows):
                pltpu.make_async_copy(ys_hbm.at[pl.ds(slot_2_ref[0, 0, r], 1)], nxt[0].at[pl.ds(r, 1)],
                                      nxt[1]).start(priority=r % 2)
        gates = gate_ref[...]
        f = gates[:, 0:1] * cur[0][0:tc, :]
        for k in range(1, TOP_K):
            f = f + gates[:, k:k + 1] * cur[0][k * tc:(k + 1) * tc, :]
        o_ref[...] = _layer_norm(ALPHA * x_ref[...] + f, g_ref[...], b_ref[...])

    for m in range(3):
        @pl.when(jnp.logical_and(i < n_steps - 2, i % 3 == m))
        def _(m=m):
            token_tile(bufs[m], bufs[(m + 2) % 3])

    for last in (n_steps - 2, n_steps - 1):
        @pl.when(i == last)
        def _(last=last):
            token_tile(bufs[last % 3], None)


def _combine_ln(x, ys, slots, gates, ln_g, ln_b, layer):
    t = x.shape[0]
    tc = COMB_TC
    n = t // tc
    slots_km = slots.reshape(n, tc, TOP_K).transpose(0, 2, 1).reshape(n, 1, TOP_K * tc)
    vec = lambda: pl.BlockSpec((None, 1, D_MODEL), lambda i: (layer, 0, 0))
    return pl.pallas_call(
        functools.partial(_combine_ln_kernel, n_steps=n),
        grid=(n,),
        in_specs=[pl.BlockSpec((1, 1, TOP_K * tc), lambda i: (i, 0, 0), memory_space=pltpu.SMEM),
                  pl.BlockSpec((1, 1, TOP_K * tc), lambda i: (jnp.minimum(i + 1, n - 1), 0, 0),
                               memory_space=pltpu.SMEM),
                  pl.BlockSpec((1, 1, TOP_K * tc), lambda i: (jnp.minimum(i + 2, n - 1), 0, 0),
                               memory_space=pltpu.SMEM),
                  pl.BlockSpec(memory_space=pl.ANY),
                  pl.BlockSpec((tc, TOP_K), lambda i: (i, 0)),
                  pl.BlockSpec((tc, D_MODEL), lambda i: (i, 0)),
                  vec(), vec()],
        out_specs=pl.BlockSpec((tc, D_MODEL), lambda i: (i, 0)),
        out_shape=jax.ShapeDtypeStruct((t, D_MODEL), F32),
        scratch_shapes=[pltpu.VMEM((TOP_K * tc, D_MODEL), F32), pltpu.VMEM((TOP_K * tc, D_MODEL), F32),
                        pltpu.VMEM((TOP_K * tc, D_MODEL), F32), pltpu.SemaphoreType.DMA((3,))],
        compiler_params=_cparams("arbitrary"),
        name="combine_ln",
    )(slots_km, slots_km, slots_km, ys, gates, x, ln_g.reshape(DEPTH, 1, D_MODEL), ln_b.reshape(DEPTH, 1, D_MODEL))


def _rope_tables(seq):
    inv = ROPE_THETA ** (-jnp.arange(0, ROT_DIM, 2, dtype=F32) / ROT_DIM)
    ang = jnp.arange(seq, dtype=F32)[:, None] * inv[None, :]
    cos, sin = jnp.cos(ang), jnp.sin(ang)
    half = ROT_DIM // 2
    rest = HEAD_DIM - ROT_DIM
    zeros_h = jnp.zeros((seq, half), F32)
    head = lambda a, b, fill: jnp.concatenate([a, b, jnp.full((seq, rest), fill, F32)], axis=1)
    tab = jnp.stack([head(cos, cos, 1.0), head(zeros_h, sin, 0.0), head(-sin, zeros_h, 0.0)])
    return jnp.tile(tab, (1, 1, LANES // HEAD_DIM))


def kernel(x, mem, a_w_in, a_conv_w, b_w_in, b_q_norm_g, b_w_uq, b_w_uq_idx, c_w_in, c_sinks, w_out, w_kv_mem,
           ln1_g, ln1_b, router_w, router_b, w_gate_up, b_gate_up, w_down, b_down, ln2_g, ln2_b):
    batch, seq, _ = x.shape
    mem_len = mem.shape[1]
    t = batch * seq
    xs = x.reshape(t, D_MODEL)
    mem2 = mem.reshape(batch * mem_len, D_MODEL)
    rope_tab = _rope_tables(seq)

    o1 = Q_RANK
    o3 = o1 + 2 * KV_WIDTH
    o5 = o3 + IDX_DIM + IDX_HEADS
    b_w = jnp.concatenate([b_w_in[:, :, :o3], b_w_in[:, :, o5:], b_w_in[:, :, o3:o5],
                           jnp.zeros(b_w_in.shape[:2] + (LANES - IDX_DIM - IDX_HEADS,), b_w_in.dtype)], axis=2)
    a_w, b_w, c_w = a_w_in.astype(BF16), b_w.astype(BF16), c_w_in.astype(BF16)
    b_wq, b_wqi = b_w_uq.astype(BF16), b_w_uq_idx.astype(BF16)

    for i in range(DEPTH):
        kind, j = i % N_MIXERS, i // N_MIXERS
        if kind == 0:
            mix, qm = _conv_mixer(xs, a_w, a_conv_w, j, seq)
        elif kind == 1:
            q, qi, k, v, ki, wi, qm = _dsa_proj(xs, b_w, b_q_norm_g, b_wq, b_wqi, j, rope_tab, seq)
            mix = _dsa_attention(q, qi, wi, k, v, ki, seq)
        else:
            q, k, v, qm = _swa_proj(xs, c_w, j, rope_tab, seq)
            mix = _swa_mixer(q, k, v, c_sinks, j, seq)
        kv_mem = _matmul(mem2, w_kv_mem, i, BF16, min(MM_TM, batch * mem_len), MM_TN)
        mem_out = _mem_attention(qm, kv_mem, seq, mem_len)
        xs = _outproj_ln(xs, mix, mem_out, w_out, ln1_g, ln1_b, i)
        top_idx, gates = _router(xs, router_w, router_b, i)
        row_ids, tile_info, n_used, slots = _dispatch_plan(top_idx)
        ys = _moe_experts(xs, row_ids, tile_info, n_used, w_gate_up, b_gate_up, w_down, b_down, i)
        xs = _combine_ln(xs, ys, slots, gates, ln2_g, ln2_b, i)
    return xs.reshape(batch, seq, D_MODEL)
```

```python
import functools

import jax
import jax.numpy as jnp
from jax import lax
from jax.experimental import pallas as pl
from jax.experimental.pallas import tpu as pltpu

F32 = jnp.float32
BF16 = jnp.bfloat16
I32 = jnp.int32

D_MODEL = 1024
DEPTH = 4
N_MIXERS = 3
HEAD_DIM = 64
MIX_HEADS = 12
MIX_WIDTH = MIX_HEADS * HEAD_DIM
MEM_HEADS = 4
MEM_WIDTH = MEM_HEADS * HEAD_DIM
KV_HEADS = 4
KV_WIDTH = KV_HEADS * HEAD_DIM
HEADS_PER_KV = MIX_HEADS // KV_HEADS
ROT_DIM = HEAD_DIM // 4
ROPE_THETA = 500000.0
CONV_WIDTH = 3
Q_RANK = 256
IDX_HEADS = 8
IDX_DIM = 64
TOPK_MAX = 256
BLOCK = 128
N_EXPERTS = 32
TOP_K = 4
D_EXPERT = 1024
SWIGLU_LIMIT = 7.0
SWIGLU_ALPHA = 1.702
ALPHA = (2 * DEPTH) ** 0.25
LN_EPS = 1e-5

LANES = 128
QK_SCALE = HEAD_DIM ** -0.5
INT_MIN = -(2 ** 31)
NEG_BIG = -1e30
VMEM_LIMIT = 48 * 1024 * 1024

MM_TM, MM_TN = 1024, 512
PROJ_TM = 512
CONV_TS = 512
MEM_TQ = 512
LN_TM = 512
ROUTER_TM = 512
MOE_TM = 256
COMB_TC = 128
SWA_TQ = 256
DSA_TQ = 256
DSA_KC = 512


def _cparams(*sem):
    return pltpu.CompilerParams(dimension_semantics=sem, vmem_limit_bytes=VMEM_LIMIT)


def _dot(a, b):
    return jnp.dot(a, b, preferred_element_type=F32)


def _dot_nt(a, b):
    return lax.dot_general(a, b, (((1,), (1,)), ((), ())), preferred_element_type=F32)


def _layer_norm(h, g, b):
    mu = jnp.mean(h, axis=-1, keepdims=True)
    d = h - mu
    var = jnp.mean(d * d, axis=-1, keepdims=True)
    return d * lax.rsqrt(var + LN_EPS) * g + b


def _rope(x, c, s1, s2):
    outs = []
    for k in range(x.shape[1] // LANES):
        xs = x[:, k * LANES:(k + 1) * LANES]
        outs.append(xs * c + pltpu.roll(xs, 8, 1) * s1 + pltpu.roll(xs, LANES - 8, 1) * s2)
    return outs[0] if len(outs) == 1 else jnp.concatenate(outs, axis=1)


def _matmul_kernel(x_ref, w_ref, o_ref):
    o_ref[...] = _dot(x_ref[...].astype(BF16), w_ref[...].astype(BF16)).astype(o_ref.dtype)


def _matmul(x, w, layer, out_dtype, tm, tn):
    m, k = x.shape
    n = w.shape[-1]
    return pl.pallas_call(
        _matmul_kernel,
        grid=(m // tm, n // tn),
        in_specs=[pl.BlockSpec((tm, k), lambda i, j: (i, 0)),
                  pl.BlockSpec((None, k, tn), lambda i, j: (layer, 0, j))],
        out_specs=pl.BlockSpec((tm, tn), lambda i, j: (i, j)),
        out_shape=jax.ShapeDtypeStruct((m, n), out_dtype),
        compiler_params=_cparams("parallel", "parallel"),
        name="proj_matmul",
    )(x, w)


def _conv_kernel(x_ref, w_ref, cw_ref, mix_ref, qm_ref, zz_ref, *, tiles_per_seq):
    ts = x_ref.shape[0]
    wb = MIX_WIDTH

    @pl.when(pl.program_id(0) % tiles_per_seq == 0)
    def _():
        zz_ref[0:8, :] = jnp.zeros((8, wb), F32)

    proj = _dot(x_ref[...].astype(BF16), w_ref[...])
    z = proj[:, 2 * wb:3 * wb] * proj[:, 0:wb]
    zz_ref[8:, :] = z
    cw = cw_ref[...]
    conv = cw[0:1, :] * zz_ref[6:6 + ts, :] + cw[1:2, :] * zz_ref[7:7 + ts, :] + cw[2:3, :] * z
    mix_ref[...] = (proj[:, wb:2 * wb] * conv).astype(mix_ref.dtype)
    qm_ref[...] = (proj[:, 3 * wb:] * QK_SCALE).astype(qm_ref.dtype)
    zz_ref[0:8, :] = zz_ref[ts:ts + 8, :]


def _conv_mixer(x, w_in_bf, conv_w, layer, seq):
    t = x.shape[0]
    ts = CONV_TS
    n_in = w_in_bf.shape[-1]
    return pl.pallas_call(
        functools.partial(_conv_kernel, tiles_per_seq=seq // ts),
        grid=(t // ts,),
        in_specs=[pl.BlockSpec((ts, D_MODEL), lambda i: (i, 0)),
                  pl.BlockSpec((None, D_MODEL, n_in), lambda i: (layer, 0, 0)),
                  pl.BlockSpec((None, CONV_WIDTH, MIX_WIDTH), lambda i: (layer, 0, 0))],
        out_specs=[pl.BlockSpec((ts, MIX_WIDTH), lambda i: (i, 0)), pl.BlockSpec((ts, MEM_WIDTH), lambda i: (i, 0))],
        out_shape=[jax.ShapeDtypeStruct((t, MIX_WIDTH), BF16), jax.ShapeDtypeStruct((t, MEM_WIDTH), BF16)],
        scratch_shapes=[pltpu.VMEM((ts + 8, MIX_WIDTH), F32)],
        compiler_params=_cparams("arbitrary"),
        name="conv_mixer",
    )(x, w_in_bf, conv_w)


def _swa_proj_kernel(x_ref, w_ref, tab_ref, q_out, k_out, v_out, qm_out):
    tab = tab_ref[...]
    c, s1, s2 = tab[0], tab[1], tab[2]
    proj = _dot(x_ref[...].astype(BF16), w_ref[...])
    o1, o2, o3 = MIX_WIDTH, MIX_WIDTH + KV_WIDTH, MIX_WIDTH + 2 * KV_WIDTH
    q_out[...] = (_rope(proj[:, :o1], c, s1, s2) * QK_SCALE).astype(BF16)
    k_out[...] = _rope(proj[:, o1:o2], c, s1, s2).astype(BF16)
    v_out[...] = proj[:, o2:o3].astype(BF16)
    qm_out[...] = (proj[:, o3:] * QK_SCALE).astype(BF16)


def _swa_proj(x, w_in_bf, layer, rope_tab, seq):
    t = x.shape[0]
    tm = PROJ_TM
    nts = seq // tm
    n_in = w_in_bf.shape[-1]
    row = lambda w: pl.BlockSpec((tm, w), lambda i: (i, 0))
    return pl.pallas_call(
        _swa_proj_kernel,
        grid=(t // tm,),
        in_specs=[row(D_MODEL),
                  pl.BlockSpec((None, D_MODEL, n_in), lambda i: (layer, 0, 0)),
                  pl.BlockSpec((3, tm, LANES), lambda i: (0, i % nts, 0))],
        out_specs=[row(MIX_WIDTH), row(KV_WIDTH), row(KV_WIDTH), row(MEM_WIDTH)],
        out_shape=[jax.ShapeDtypeStruct((t, MIX_WIDTH), BF16), jax.ShapeDtypeStruct((t, KV_WIDTH), BF16),
                   jax.ShapeDtypeStruct((t, KV_WIDTH), BF16), jax.ShapeDtypeStruct((t, MEM_WIDTH), BF16)],
        compiler_params=_cparams("parallel"),
        name="swa_proj",
    )(x, w_in_bf, rope_tab)


def _swa_kernel(q_ref, kc_ref, kp_ref, vc_ref, vp_ref, sink_ref, o_ref, *, tiles_per_seq):
    tq = q_ref.shape[0]
    first = (pl.program_id(0) % tiles_per_seq) == 0
    q = q_ref[...]
    k = jnp.concatenate([kp_ref[...], kc_ref[...]], axis=0)
    v = jnp.concatenate([vp_ref[...], vc_ref[...]], axis=0)
    qi = lax.broadcasted_iota(I32, (tq, BLOCK + tq), 0)
    r = lax.broadcasted_iota(I32, (tq, BLOCK + tq), 1)
    keep = jnp.where(r > qi, jnp.where(r <= qi + BLOCK, 1, 0), 0)
    keep = jnp.where(first, jnp.where(r >= BLOCK, keep, 0), keep)
    bias = jnp.where(keep > 0, 0.0, -jnp.inf)
    bias = jnp.concatenate([bias] * HEADS_PER_KV, axis=0)
    sinks = sink_ref[...]
    for g in range(KV_HEADS):
        heads = range(g * HEADS_PER_KV, (g + 1) * HEADS_PER_KV)
        kv_cols = slice(g * HEAD_DIM, (g + 1) * HEAD_DIM)
        qg = jnp.concatenate([q[:, h * HEAD_DIM:(h + 1) * HEAD_DIM] for h in heads], axis=0)
        sink = jnp.concatenate([jnp.broadcast_to(sinks[0:1, h:h + 1], (tq, 1)) for h in heads], axis=0)
        logits = _dot_nt(qg, k[:, kv_cols]) + bias
        m = jnp.maximum(jnp.max(logits, axis=-1, keepdims=True), sink)
        e = jnp.exp(logits - m)
        p = e / (jnp.sum(e, axis=-1, keepdims=True) + jnp.exp(sink - m))
        out = _dot(p.astype(BF16), v[:, kv_cols])
        for n, h in enumerate(heads):
            o_ref[:, h * HEAD_DIM:(h + 1) * HEAD_DIM] = out[n * tq:(n + 1) * tq].astype(o_ref.dtype)


def _swa_mixer(q, k, v, sinks, layer, seq):
    t = q.shape[0]
    tq = SWA_TQ
    per_tile = tq // BLOCK
    prev_blk = lambda i: jnp.maximum(i * per_tile - 1, 0)
    return pl.pallas_call(
        functools.partial(_swa_kernel, tiles_per_seq=seq // tq),
        grid=(t // tq,),
        in_specs=[pl.BlockSpec((tq, MIX_WIDTH), lambda i: (i, 0)),
                  pl.BlockSpec((tq, KV_WIDTH), lambda i: (i, 0)),
                  pl.BlockSpec((BLOCK, KV_WIDTH), lambda i: (prev_blk(i), 0)),
                  pl.BlockSpec((tq, KV_WIDTH), lambda i: (i, 0)),
                  pl.BlockSpec((BLOCK, KV_WIDTH), lambda i: (prev_blk(i), 0)),
                  pl.BlockSpec((None, 1, MIX_HEADS), lambda i: (layer, 0, 0))],
        out_specs=pl.BlockSpec((tq, MIX_WIDTH), lambda i: (i, 0)),
        out_shape=jax.ShapeDtypeStruct((t, MIX_WIDTH), BF16),
        compiler_params=_cparams("parallel"),
        name="swa_mixer",
    )(q, k, k, v, v, sinks.reshape(sinks.shape[0], 1, MIX_HEADS))


def _dsa_proj_kernel(x_ref, w_ref, g_ref, wq_ref, wqi_ref, tab_ref,
                     q_out, qi_out, k_out, v_out, ki_out, wi_out, qm_out):
    tab = tab_ref[...]
    c, s1, s2 = tab[0], tab[1], tab[2]
    proj = _dot(x_ref[...].astype(BF16), w_ref[...])
    o1, o2, o3, o4 = Q_RANK, Q_RANK + KV_WIDTH, Q_RANK + 2 * KV_WIDTH, Q_RANK + 2 * KV_WIDTH + MEM_WIDTH
    cq = proj[:, :o1]
    cqn = cq * lax.rsqrt(jnp.mean(cq * cq, axis=-1, keepdims=True) + LN_EPS) * g_ref[...]
    cqn = cqn.astype(BF16)
    q_out[...] = (_rope(_dot(cqn, wq_ref[...]), c, s1, s2) * QK_SCALE).astype(BF16)
    qi_out[...] = (_rope(_dot(cqn, wqi_ref[...]), c, s1, s2) * (IDX_DIM ** -0.5)).astype(BF16)
    k_out[...] = _rope(proj[:, o1:o2], c, s1, s2).astype(BF16)
    v_out[...] = proj[:, o2:o3].astype(BF16)
    qm_out[...] = (proj[:, o3:o4] * QK_SCALE).astype(BF16)
    kiw = proj[:, o4:]
    ki_out[...] = _rope(kiw, c, s1, s2).astype(BF16)
    wi_out[...] = kiw


def _dsa_proj(x, w_in_bf, q_norm_g, w_uq_bf, w_uq_idx_bf, layer, rope_tab, seq):
    t = x.shape[0]
    tm = PROJ_TM
    nts = seq // tm
    qi_w = IDX_HEADS * IDX_DIM
    n_in = w_in_bf.shape[-1]
    row = lambda w: pl.BlockSpec((tm, w), lambda i: (i, 0))
    return pl.pallas_call(
        _dsa_proj_kernel,
        grid=(t // tm,),
        in_specs=[row(D_MODEL),
                  pl.BlockSpec((None, D_MODEL, n_in), lambda i: (layer, 0, 0)),
                  pl.BlockSpec((None, 1, Q_RANK), lambda i: (layer, 0, 0)),
                  pl.BlockSpec((None, Q_RANK, MIX_WIDTH), lambda i: (layer, 0, 0)),
                  pl.BlockSpec((None, Q_RANK, qi_w), lambda i: (layer, 0, 0)),
                  pl.BlockSpec((3, tm, LANES), lambda i: (0, i % nts, 0))],
        out_specs=[row(MIX_WIDTH), row(qi_w), row(KV_WIDTH), row(KV_WIDTH), row(LANES), row(LANES), row(MEM_WIDTH)],
        out_shape=[jax.ShapeDtypeStruct((t, MIX_WIDTH), BF16), jax.ShapeDtypeStruct((t, qi_w), BF16),
                   jax.ShapeDtypeStruct((t, KV_WIDTH), BF16), jax.ShapeDtypeStruct((t, KV_WIDTH), BF16),
                   jax.ShapeDtypeStruct((t, LANES), BF16), jax.ShapeDtypeStruct((t, LANES), F32),
                   jax.ShapeDtypeStruct((t, MEM_WIDTH), BF16)],
        compiler_params=_cparams("parallel"),
        name="dsa_proj",
    )(x, w_in_bf, q_norm_g.reshape(q_norm_g.shape[0], 1, Q_RANK), w_uq_bf, w_uq_idx_bf, rope_tab)


def _lane_fold(x, op, acc):
    for t in range(x.shape[1] // LANES):
        acc = op(acc, x[:, t * LANES:(t + 1) * LANES])
    return acc


def _key_to_float(key):
    return lax.bitcast_convert_type(key ^ (lax.shift_right_arithmetic(key, 31) & 0x7FFFFFFF), F32)


KEY_NEG_INF = INT_MIN + 2 ** 23 - 1


def _dsa_kernel(q_ref, qi_ref, wi_ref, k_ref, v_ref, ki_ref, o_ref, score_ref, score_t_ref, s_ref, m_ref, l_ref,
                acc_ref, *, tiles_per_seq, n_sel):
    tq = q_ref.shape[0]
    seq = k_ref.shape[0]
    kc = min(DSA_KC, seq)
    jt = pl.program_id(0) % tiles_per_seq
    n_chunks = (jt * tq + tq + kc - 1) // kc
    qpos = jt * tq + lax.broadcasted_iota(I32, (tq, 1), 0)

    def chunk_start(c):
        return pl.multiple_of(c * kc, kc)

    def key_positions(c):
        return c * kc + lax.broadcasted_iota(I32, (1, kc), 1)

    wi = wi_ref[...][:, IDX_DIM:IDX_DIM + IDX_HEADS] * (IDX_HEADS ** -0.5)
    qi = qi_ref[...]
    qi_heads = [qi[:, h * IDX_DIM:(h + 1) * IDX_DIM] for h in range(IDX_HEADS)]
    wi_heads = [wi[:, h:h + 1] for h in range(IDX_HEADS)]

    def score_chunk(c, carry):
        off = chunk_start(c)
        kch = ki_ref[pl.ds(off, kc), 0:IDX_DIM]
        score = jnp.zeros((tq, kc), F32)
        for h in range(IDX_HEADS):
            score = score + wi_heads[h] * jnp.maximum(_dot_nt(qi_heads[h], kch), 0.0)
        score = jnp.where(key_positions(c) <= qpos, score, -jnp.inf)
        score_ref[:, pl.ds(off, kc)] = score
        score_t_ref[pl.ds(off, kc), :] = score.T
        return carry

    lax.fori_loop(0, n_chunks, score_chunk, 0)

    def key_rows(c):
        return c * kc + lax.broadcasted_iota(I32, (kc, 1), 0)

    def count(pred):
        def body(c, acc):
            ones = jnp.where(pred(score_t_ref[pl.ds(chunk_start(c), kc), :], c), 1.0, 0.0)
            parts = [acc, None, None, None]
            for r in range(kc // 8):
                row = ones[r * 8:(r + 1) * 8, :]
                parts[r % 4] = row if parts[r % 4] is None else parts[r % 4] + row
            return (parts[0] + parts[1]) + (parts[2] + parts[3])
        acc = lax.fori_loop(0, n_chunks, body, jnp.zeros((8, tq), F32))
        return jnp.sum(acc, axis=0, keepdims=True)

    def value_bit(it, lo):
        cand = lo + lax.shift_left(jnp.int32(1), 31 - it)
        cand_f = _key_to_float(cand)
        cnt = count(lambda sc, c: sc >= cand_f)
        take = jnp.logical_or(cnt >= n_sel, cand < KEY_NEG_INF)
        return jnp.where(take, cand, lo)

    lo = lax.fori_loop(0, 32, value_bit, jnp.full((1, tq), INT_MIN, I32))
    thr_row = _key_to_float(jnp.maximum(lo, KEY_NEG_INF))
    need = n_sel - count(lambda sc, c: sc > thr_row)
    n_eq = count(lambda sc, c: sc == thr_row)
    surplus = jnp.where(n_eq > need, jnp.where(lo > KEY_NEG_INF, 1, 0), 0)

    def tie_cut():
        idx_bits = max(seq - 1, 1).bit_length()

        def index_bit(it, cut):
            cand = cut + lax.shift_left(jnp.int32(1), idx_bits - 1 - it)
            cnt = count(lambda sc, c: jnp.logical_and(sc == thr_row, key_rows(c) < cand))
            return jnp.where(cnt < need, cand, cut)

        cut = lax.fori_loop(0, idx_bits, index_bit, jnp.zeros((1, tq), I32))
        return jnp.where(surplus > 0, cut, seq)

    cut_row = lax.cond(jnp.max(surplus) > 0, tie_cut, lambda: jnp.full((1, tq), seq, I32))

    def to_column(row):
        return jnp.broadcast_to(row, (LANES, tq)).T[:, 0:1]

    thr = to_column(thr_row)
    cut = to_column(cut_row.astype(F32)).astype(I32)

    def bias_chunk(c, carry):
        off = chunk_start(c)
        sc = score_ref[:, pl.ds(off, kc)]
        kpos = key_positions(c)
        tie = jnp.logical_and(sc == thr, kpos <= cut)
        sel = jnp.logical_and(kpos <= qpos, jnp.logical_or(sc > thr, tie))
        score_ref[:, pl.ds(off, kc)] = jnp.where(sel, 0.0, NEG_BIG)
        return carry

    lax.fori_loop(0, n_chunks, bias_chunk, 0)

    q = q_ref[...]
    for g in range(KV_HEADS):
        heads = range(g * HEADS_PER_KV, (g + 1) * HEADS_PER_KV)
        qg = jnp.concatenate([q[:, h * HEAD_DIM:(h + 1) * HEAD_DIM] for h in heads], axis=0)
        kv_cols = slice(g * HEAD_DIM, (g + 1) * HEAD_DIM)
        m_ref[...] = jnp.full(m_ref.shape, NEG_BIG, F32)
        l_ref[...] = jnp.zeros(l_ref.shape, F32)
        acc_ref[...] = jnp.zeros(acc_ref.shape, F32)

        def logits_chunk(c, carry, qg=qg, kv_cols=kv_cols):
            off = chunk_start(c)
            bias = score_ref[:, pl.ds(off, kc)]
            s = _dot_nt(qg, k_ref[pl.ds(off, kc), kv_cols]) + jnp.concatenate([bias] * HEADS_PER_KV, axis=0)
            s_ref[:, pl.ds(off, kc)] = s
            m_ref[...] = _lane_fold(s, jnp.maximum, m_ref[...])
            return carry

        lax.fori_loop(0, n_chunks, logits_chunk, 0)
        m = jnp.max(m_ref[...], axis=1, keepdims=True)

        def pv_chunk(c, carry, m=m, kv_cols=kv_cols):
            off = chunk_start(c)
            p = jnp.exp(s_ref[:, pl.ds(off, kc)] - m)
            l_ref[...] = _lane_fold(p, jnp.add, l_ref[...])
            acc_ref[...] += _dot(p.astype(BF16), v_ref[pl.ds(off, kc), kv_cols])
            return carry

        lax.fori_loop(0, n_chunks, pv_chunk, 0)
        out = acc_ref[...] / jnp.sum(l_ref[...], axis=1, keepdims=True)
        for r, h in enumerate(heads):
            o_ref[:, h * HEAD_DIM:(h + 1) * HEAD_DIM] = out[r * tq:(r + 1) * tq].astype(o_ref.dtype)


def _dsa_attention(q, qi, wi, k, v, ki, seq):
    t = q.shape[0]
    tq = DSA_TQ
    nts = seq // tq
    n_sel = min(TOPK_MAX, seq // 4)
    per_seq = lambda w: pl.BlockSpec((seq, w), lambda i: (i // nts, 0))
    return pl.pallas_call(
        functools.partial(_dsa_kernel, tiles_per_seq=nts, n_sel=n_sel),
        grid=(t // tq,),
        in_specs=[pl.BlockSpec((tq, MIX_WIDTH), lambda i: (i, 0)),
                  pl.BlockSpec((tq, IDX_HEADS * IDX_DIM), lambda i: (i, 0)),
                  pl.BlockSpec((tq, LANES), lambda i: (i, 0)),
                  per_seq(KV_WIDTH), per_seq(KV_WIDTH), per_seq(LANES)],
        out_specs=pl.BlockSpec((tq, MIX_WIDTH), lambda i: (i, 0)),
        out_shape=jax.ShapeDtypeStruct((t, MIX_WIDTH), BF16),
        scratch_shapes=[pltpu.VMEM((tq, seq), F32), pltpu.VMEM((seq, tq), F32),
                        pltpu.VMEM((HEADS_PER_KV * tq, seq), F32),
                        pltpu.VMEM((HEADS_PER_KV * tq, LANES), F32),
                        pltpu.VMEM((HEADS_PER_KV * tq, LANES), F32),
                        pltpu.VMEM((HEADS_PER_KV * tq, HEAD_DIM), F32)],
        compiler_params=_cparams("parallel"),
        name="dsa_attention",
    )(q, qi, wi, k, v, ki)


def _mem_attn_kernel(q_ref, k_ref, v_ref, o_ref):
    q = q_ref[...]
    for h in range(MEM_HEADS):
        sl = slice(h * HEAD_DIM, (h + 1) * HEAD_DIM)
        logits = _dot_nt(q[:, sl], k_ref[:, sl])
        e = jnp.exp(logits - jnp.max(logits, axis=-1, keepdims=True))
        p = e / jnp.sum(e, axis=-1, keepdims=True)
        o_ref[:, sl] = _dot(p.astype(BF16), v_ref[:, sl]).astype(o_ref.dtype)


def _mem_attention(qm, kv_mem, seq, mem_len):
    t = qm.shape[0]
    tq = MEM_TQ
    nqs = seq // tq
    return pl.pallas_call(
        _mem_attn_kernel,
        grid=(t // tq,),
        in_specs=[pl.BlockSpec((tq, MEM_WIDTH), lambda i: (i, 0)),
                  pl.BlockSpec((mem_len, MEM_WIDTH), lambda i: (i // nqs, 0)),
                  pl.BlockSpec((mem_len, MEM_WIDTH), lambda i: (i // nqs, 1))],
        out_specs=pl.BlockSpec((tq, MEM_WIDTH), lambda i: (i, 0)),
        out_shape=jax.ShapeDtypeStruct((t, MEM_WIDTH), BF16),
        compiler_params=_cparams("parallel"),
        name="mem_attention",
    )(qm, kv_mem, kv_mem)


def _outproj_ln_kernel(x_ref, mix_ref, mo_ref, w_ref, g_ref, b_ref, o_ref, wbf_ref):
    @pl.when(pl.program_id(0) == 0)
    def _():
        wbf_ref[...] = w_ref[...].astype(BF16)

    y = _dot(mix_ref[...], wbf_ref[0:MIX_WIDTH, :]) + _dot(mo_ref[...], wbf_ref[MIX_WIDTH:, :])
    o_ref[...] = _layer_norm(ALPHA * x_ref[...] + y, g_ref[...], b_ref[...])


def _outproj_ln(x, mix, mem_out, w_out, ln_g, ln_b, layer):
    t = x.shape[0]
    tm = LN_TM
    vec = lambda: pl.BlockSpec((None, 1, D_MODEL), lambda i: (layer, 0, 0))
    return pl.pallas_call(
        _outproj_ln_kernel,
        grid=(t // tm,),
        in_specs=[pl.BlockSpec((tm, D_MODEL), lambda i: (i, 0)),
                  pl.BlockSpec((tm, MIX_WIDTH), lambda i: (i, 0)),
                  pl.BlockSpec((tm, MEM_WIDTH), lambda i: (i, 0)),
                  pl.BlockSpec((None, D_MODEL, D_MODEL), lambda i: (layer, 0, 0)),
                  vec(), vec()],
        out_specs=pl.BlockSpec((tm, D_MODEL), lambda i: (i, 0)),
        out_shape=jax.ShapeDtypeStruct((t, D_MODEL), F32),
        scratch_shapes=[pltpu.VMEM((D_MODEL, D_MODEL), BF16)],
        compiler_params=_cparams("arbitrary"),
        name="outproj_ln",
    )(x, mix, mem_out, w_out, ln_g.reshape(DEPTH, 1, D_MODEL), ln_b.reshape(DEPTH, 1, D_MODEL))


def _router_kernel(x_ref, w_ref, b_ref, tri_ref, idx_ref, gate_ref, rank_ref, count_ref, seen_ref):
    @pl.when(pl.program_id(0) == 0)
    def _():
        seen_ref[...] = jnp.zeros(seen_ref.shape, F32)

    x = x_ref[...]
    w = w_ref[...]
    x_hi = x.astype(BF16)
    x_lo = (x - x_hi.astype(F32)).astype(BF16)
    w_hi = w.astype(BF16)
    w_lo = (w - w_hi.astype(F32)).astype(BF16)
    hi = _dot(x_hi, jnp.concatenate([w_hi, w_lo], axis=1))
    logits = (hi[:, :N_EXPERTS] + (hi[:, N_EXPERTS:] + _dot(x_lo, w_hi))) + b_ref[...]
    lane = lax.broadcasted_iota(I32, logits.shape, 1).astype(F32)
    vals, idxs = [], []
    for _ in range(TOP_K):
        m = jnp.max(logits, axis=-1, keepdims=True)
        ix = jnp.min(jnp.where(logits == m, lane, float(N_EXPERTS)), axis=-1, keepdims=True)
        vals.append(m)
        idxs.append(ix)
        logits = jnp.where(lane == ix, -jnp.inf, logits)
    e = jnp.exp(jnp.concatenate(vals, axis=1) - vals[0])
    idx_ref[...] = jnp.concatenate(idxs, axis=1).astype(I32)
    gate_ref[...] = e / jnp.sum(e, axis=-1, keepdims=True)
    hits = [jnp.where(lane == ix, 1.0, 0.0) for ix in idxs]
    routed = (hits[0] + hits[1]) + (hits[2] + hits[3])
    before = _dot(tri_ref[...], routed.astype(BF16)) + seen_ref[...]
    rank_ref[...] = jnp.concatenate([jnp.sum(hit * before, axis=-1, keepdims=True) for hit in hits],
                                    axis=1).astype(I32)
    seen_ref[...] += jnp.sum(routed, axis=0, keepdims=True)
    count_ref[...] = seen_ref[...].astype(I32)


def _router(x, router_w, router_b, layer):
    t = x.shape[0]
    tm = ROUTER_TM
    tri = jnp.tri(tm, k=-1, dtype=BF16)
    per_token = lambda: pl.BlockSpec((tm, TOP_K), lambda i: (i, 0))
    return pl.pallas_call(
        _router_kernel,
        grid=(t // tm,),
        in_specs=[pl.BlockSpec((tm, D_MODEL), lambda i: (i, 0)),
                  pl.BlockSpec((None, D_MODEL, N_EXPERTS), lambda i: (layer, 0, 0)),
                  pl.BlockSpec((None, 1, N_EXPERTS), lambda i: (layer, 0, 0)),
                  pl.BlockSpec((tm, tm), lambda i: (0, 0))],
        out_specs=[per_token(), per_token(), per_token(), pl.BlockSpec((1, N_EXPERTS), lambda i: (0, 0))],
        out_shape=[jax.ShapeDtypeStruct((t, TOP_K), I32), jax.ShapeDtypeStruct((t, TOP_K), F32),
                   jax.ShapeDtypeStruct((t, TOP_K), I32), jax.ShapeDtypeStruct((1, N_EXPERTS), I32)],
        scratch_shapes=[pltpu.VMEM((1, N_EXPERTS), F32)],
        compiler_params=_cparams("arbitrary"),
        name="router",
    )(x, router_w, router_b.reshape(DEPTH, 1, N_EXPERTS), tri)


def _dispatch_plan(top_idx, rank, counts):
    t = top_idx.shape[0]
    n_assign = t * TOP_K
    tm = MOE_TM
    n_tiles = n_assign // tm + N_EXPERTS + 1
    e = top_idx.reshape(n_assign)
    experts = jnp.arange(N_EXPERTS, dtype=I32)
    counts = counts.reshape(N_EXPERTS)
    padded = ((counts + tm - 1) // tm) * tm
    group_end = jnp.cumsum(padded)
    group_start = group_end - padded
    slot = rank.reshape(n_assign) + jnp.sum(jnp.where(e[:, None] == experts[None, :], group_start[None, :], 0), axis=1)
    n_used = group_end[-1] // tm
    tile = jnp.arange(n_tiles, dtype=I32)
    tile_expert = jnp.minimum(jnp.sum((tile[:, None] * tm >= group_end[None, :]).astype(I32), axis=1), N_EXPERTS - 1)
    last_expert = jnp.sum(jnp.where(tile == n_used - 1, tile_expert, 0))
    tile_expert = jnp.where(tile < n_used, tile_expert, last_expert).astype(I32)
    stride = 2 * n_assign
    pad = jnp.arange(tm, dtype=I32)
    pad_keys = jnp.where(pad[None, :] < (padded - counts)[:, None],
                         experts[:, None] * stride + n_assign + pad[None, :], jnp.iinfo(I32).max)
    keys = jnp.sort(jnp.concatenate([e * stride + jnp.arange(n_assign, dtype=I32), pad_keys.reshape(-1)]))
    index = keys % stride
    row_ids = jnp.where(index < n_assign, index // TOP_K, 0)
    row_ids = jnp.concatenate([row_ids, jnp.zeros(((n_tiles - n_assign // tm - N_EXPERTS) * tm,), I32)])
    nonempty = counts > 0
    later = jnp.where(nonempty[None, :] & (experts[None, :] > experts[:, None]), experts[None, :], N_EXPERTS)
    next_expert = jnp.min(later, axis=1)
    next_expert = jnp.where(next_expert < N_EXPERTS, next_expert, -1)
    group_parity = (jnp.cumsum(nonempty.astype(I32)) - 1) % 2
    info = jnp.stack([tile_expert, next_expert[tile_expert], group_parity[tile_expert]]).astype(I32)
    return row_ids.reshape(n_tiles, 1, tm), info, n_used.reshape(1).astype(I32), slot.reshape(t, TOP_K)


def _gather_rows(src_hbm, ids_ref, n_rows, dst_ref, sem):
    def body(r, carry):
        pltpu.make_async_copy(src_hbm.at[pl.ds(ids_ref[0, 0, r], 1)], dst_ref.at[pl.ds(r, 1)], sem).start()
        return carry
    lax.fori_loop(0, n_rows, body, 0, unroll=8)


def _wait_rows(dst_ref, sem):
    pltpu.make_async_copy(dst_ref, dst_ref, sem).wait()


def _moe_kernel(info_ref, nu_ref, ids_ref, ids_1_ref, ids_2_ref, x_hbm, wgu_hbm, bgu_ref, wd_hbm, bd_ref, o_ref,
                xbuf0, xbuf1, xbuf2, sems, wgu_f32, wd_f32, wsems, wgu_bf, wd_bf, *, layer):
    j = pl.program_id(0)
    n_used = nu_ref[0]
    tm = xbuf0.shape[0]
    bufs = ((xbuf0, sems.at[0]), (xbuf1, sems.at[1]), (xbuf2, sems.at[2]))
    expert, next_expert, wslot = info_ref[0, j], info_ref[1, j], info_ref[2, j]

    def weight_copies(e, slot):
        return (pltpu.make_async_copy(wgu_hbm.at[layer, e], wgu_f32.at[slot], wsems.at[0, slot]),
                pltpu.make_async_copy(wd_hbm.at[layer, e], wd_f32.at[slot], wsems.at[1, slot]))

    @pl.when(j == 0)
    def _():
        for cp in weight_copies(expert, wslot):
            cp.start()
        _gather_rows(x_hbm, ids_ref, tm, xbuf0, sems.at[0])
        _gather_rows(x_hbm, ids_1_ref, tm, xbuf1, sems.at[1])

    def expert_tile(cur, nxt):
        _wait_rows(*cur)

        @pl.when(jnp.logical_or(j == 0, expert != info_ref[0, jnp.maximum(j - 1, 0)]))
        def _():
            for cp in weight_copies(expert, wslot):
                cp.wait()

            @pl.when(next_expert >= 0)
            def _():
                for cp in weight_copies(next_expert, 1 - wslot):
                    cp.start()

            wgu_bf[...] = wgu_f32[wslot].astype(BF16)
            wd_bf[...] = wd_f32[wslot].astype(BF16)

        for r in range(tm):
            pltpu.make_async_copy(x_hbm.at[pl.ds(ids_2_ref[0, 0, r], 1)], nxt[0].at[pl.ds(r, 1)],
                                  nxt[1]).start(priority=r % 2)
        h = _dot(cur[0][...].astype(BF16), wgu_bf[...]) + bgu_ref[...]
        gate = jnp.minimum(h[:, :D_EXPERT], SWIGLU_LIMIT)
        up = jnp.clip(h[:, D_EXPERT:], -SWIGLU_LIMIT, SWIGLU_LIMIT)
        act = (up + 1.0) * gate * jax.nn.sigmoid(SWIGLU_ALPHA * gate)
        o_ref[...] = _dot(act.astype(BF16), wd_bf[...]) + bd_ref[...]

    for m in range(3):
        @pl.when(jnp.logical_and(j < n_used, j % 3 == m))
        def _(m=m):
            expert_tile(bufs[m], bufs[(m + 2) % 3])

        @pl.when(jnp.logical_and(jnp.logical_and(j >= n_used, j <= n_used + 1), j % 3 == m))
        def _(m=m):
            _wait_rows(*bufs[m])

    @pl.when(j >= n_used)
    def _():
        o_ref[...] = jnp.zeros_like(o_ref)


def _moe_experts(x, row_ids, tile_info, n_used, w_gate_up, b_gate_up, w_down, b_down, layer):
    n_tiles = row_ids.shape[0]
    tm = MOE_TM
    grid_spec = pltpu.PrefetchScalarGridSpec(
        num_scalar_prefetch=2,
        grid=(n_tiles,),
        in_specs=[pl.BlockSpec((1, 1, tm), lambda j, info, nu: (j, 0, 0), memory_space=pltpu.SMEM),
                  pl.BlockSpec((1, 1, tm), lambda j, info, nu: (jnp.minimum(j + 1, n_tiles - 1), 0, 0),
                               memory_space=pltpu.SMEM),
                  pl.BlockSpec((1, 1, tm), lambda j, info, nu: (jnp.minimum(j + 2, n_tiles - 1), 0, 0),
                               memory_space=pltpu.SMEM),
                  pl.BlockSpec(memory_space=pl.ANY),
                  pl.BlockSpec(memory_space=pl.ANY),
                  pl.BlockSpec((None, None, 1, 2 * D_EXPERT), lambda j, info, nu: (layer, info[0, j], 0, 0)),
                  pl.BlockSpec(memory_space=pl.ANY),
                  pl.BlockSpec((None, None, 1, D_MODEL), lambda j, info, nu: (layer, info[0, j], 0, 0))],
        out_specs=pl.BlockSpec((tm, D_MODEL), lambda j, info, nu: (j, 0)),
        scratch_shapes=[pltpu.VMEM((tm, D_MODEL), F32), pltpu.VMEM((tm, D_MODEL), F32), pltpu.VMEM((tm, D_MODEL), F32),
                        pltpu.SemaphoreType.DMA((3,)),
                        pltpu.VMEM((2, D_MODEL, 2 * D_EXPERT), F32), pltpu.VMEM((2, D_EXPERT, D_MODEL), F32),
                        pltpu.SemaphoreType.DMA((2, 2)),
                        pltpu.VMEM((D_MODEL, 2 * D_EXPERT), BF16), pltpu.VMEM((D_EXPERT, D_MODEL), BF16)],
    )
    return pl.pallas_call(
        functools.partial(_moe_kernel, layer=layer),
        grid_spec=grid_spec,
        out_shape=jax.ShapeDtypeStruct((n_tiles * tm, D_MODEL), F32),
        compiler_params=_cparams("arbitrary"),
        name="moe_experts",
    )(tile_info, n_used, row_ids, row_ids, row_ids, x, w_gate_up,
      b_gate_up.reshape(DEPTH, N_EXPERTS, 1, 2 * D_EXPERT), w_down, b_down.reshape(DEPTH, N_EXPERTS, 1, D_MODEL))


def _combine_ln_kernel(slot_ref, slot_1_ref, slot_2_ref, ys_hbm, gate_ref, x_ref, g_ref, b_ref, o_ref,
                       buf0, buf1, buf2, sems, *, n_steps):
    i = pl.program_id(0)
    rows = buf0.shape[0]
    tc = rows // TOP_K
    bufs = ((buf0, sems.at[0]), (buf1, sems.at[1]), (buf2, sems.at[2]))

    @pl.when(i == 0)
    def _():
        _gather_rows(ys_hbm, slot_ref, rows, buf0, sems.at[0])
        _gather_rows(ys_hbm, slot_1_ref, rows, buf1, sems.at[1])

    def token_tile(cur, nxt):
        _wait_rows(*cur)
        if nxt is not None:
            for r in range(rows):
                pltpu.make_async_copy(ys_hbm.at[pl.ds(slot_2_ref[0, 0, r], 1)], nxt[0].at[pl.ds(r, 1)],
                                      nxt[1]).start(priority=r % 2)
        gates = gate_ref[...]
        f = gates[:, 0:1] * cur[0][0:tc, :]
        for k in range(1, TOP_K):
            f = f + gates[:, k:k + 1] * cur[0][k * tc:(k + 1) * tc, :]
        o_ref[...] = _layer_norm(ALPHA * x_ref[...] + f, g_ref[...], b_ref[...])

    for m in range(3):
        @pl.when(jnp.logical_and(i < n_steps - 2, i % 3 == m))
        def _(m=m):
            token_tile(bufs[m], bufs[(m + 2) % 3])

    for last in (n_steps - 2, n_steps - 1):
        @pl.when(i == last)
        def _(last=last):
            token_tile(bufs[last % 3], None)


def _combine_ln(x, ys, slots, gates, ln_g, ln_b, layer):
    t = x.shape[0]
    tc = COMB_TC
    n = t // tc
    slots_km = slots.reshape(n, tc, TOP_K).transpose(0, 2, 1).reshape(n, 1, TOP_K * tc)
    vec = lambda: pl.BlockSpec((None, 1, D_MODEL), lambda i: (layer, 0, 0))
    return pl.pallas_call(
        functools.partial(_combine_ln_kernel, n_steps=n),
        grid=(n,),
        in_specs=[pl.BlockSpec((1, 1, TOP_K * tc), lambda i: (i, 0, 0), memory_space=pltpu.SMEM),
                  pl.BlockSpec((1, 1, TOP_K * tc), lambda i: (jnp.minimum(i + 1, n - 1), 0, 0),
                               memory_space=pltpu.SMEM),
                  pl.BlockSpec((1, 1, TOP_K * tc), lambda i: (jnp.minimum(i + 2, n - 1), 0, 0),
                               memory_space=pltpu.SMEM),
                  pl.BlockSpec(memory_space=pl.ANY),
                  pl.BlockSpec((tc, TOP_K), lambda i: (i, 0)),
                  pl.BlockSpec((tc, D_MODEL), lambda i: (i, 0)),
                  vec(), vec()],
        out_specs=pl.BlockSpec((tc, D_MODEL), lambda i: (i, 0)),
        out_shape=jax.ShapeDtypeStruct((t, D_MODEL), F32),
        scratch_shapes=[pltpu.VMEM((TOP_K * tc, D_MODEL), F32), pltpu.VMEM((TOP_K * tc, D_MODEL), F32),
                        pltpu.VMEM((TOP_K * tc, D_MODEL), F32), pltpu.SemaphoreType.DMA((3,))],
        compiler_params=_cparams("arbitrary"),
        name="combine_ln",
    )(slots_km, slots_km, slots_km, ys, gates, x, ln_g.reshape(DEPTH, 1, D_MODEL), ln_b.reshape(DEPTH, 1, D_MODEL))


def _rope_tables(seq):
    inv = ROPE_THETA ** (-jnp.arange(0, ROT_DIM, 2, dtype=F32) / ROT_DIM)
    ang = jnp.arange(seq, dtype=F32)[:, None] * inv[None, :]
    cos, sin = jnp.cos(ang), jnp.sin(ang)
    half = ROT_DIM // 2
    rest = HEAD_DIM - ROT_DIM
    zeros_h = jnp.zeros((seq, half), F32)
    head = lambda a, b, fill: jnp.concatenate([a, b, jnp.full((seq, rest), fill, F32)], axis=1)
    tab = jnp.stack([head(cos, cos, 1.0), head(zeros_h, sin, 0.0), head(-sin, zeros_h, 0.0)])
    return jnp.tile(tab, (1, 1, LANES // HEAD_DIM))


def kernel(x, mem, a_w_in, a_conv_w, b_w_in, b_q_norm_g, b_w_uq, b_w_uq_idx, c_w_in, c_sinks, w_out, w_kv_mem,
           ln1_g, ln1_b, router_w, router_b, w_gate_up, b_gate_up, w_down, b_down, ln2_g, ln2_b):
    batch, seq, _ = x.shape
    mem_len = mem.shape[1]
    t = batch * seq
    xs = x.reshape(t, D_MODEL)
    mem2 = mem.reshape(batch * mem_len, D_MODEL)
    rope_tab = _rope_tables(seq)

    o1 = Q_RANK
    o3 = o1 + 2 * KV_WIDTH
    o5 = o3 + IDX_DIM + IDX_HEADS
    b_w = jnp.concatenate([b_w_in[:, :, :o3], b_w_in[:, :, o5:], b_w_in[:, :, o3:o5],
                           jnp.zeros(b_w_in.shape[:2] + (LANES - IDX_DIM - IDX_HEADS,), b_w_in.dtype)], axis=2)
    a_w, b_w, c_w = a_w_in.astype(BF16), b_w.astype(BF16), c_w_in.astype(BF16)
    b_wq, b_wqi = b_w_uq.astype(BF16), b_w_uq_idx.astype(BF16)

    for i in range(DEPTH):
        kind, j = i % N_MIXERS, i // N_MIXERS
        if kind == 0:
            mix, qm = _conv_mixer(xs, a_w, a_conv_w, j, seq)
        elif kind == 1:
            q, qi, k, v, ki, wi, qm = _dsa_proj(xs, b_w, b_q_norm_g, b_wq, b_wqi, j, rope_tab, seq)
            mix = _dsa_attention(q, qi, wi, k, v, ki, seq)
        else:
            q, k, v, qm = _swa_proj(xs, c_w, j, rope_tab, seq)
            mix = _swa_mixer(q, k, v, c_sinks, j, seq)
        kv_mem = _matmul(mem2, w_kv_mem, i, BF16, min(MM_TM, batch * mem_len), MM_TN)
        mem_out = _mem_attention(qm, kv_mem, seq, mem_len)
        xs = _outproj_ln(xs, mix, mem_out, w_out, ln1_g, ln1_b, i)
        top_idx, gates, rank, counts = _router(xs, router_w, router_b, i)
        row_ids, tile_info, n_used, slots = _dispatch_plan(top_idx, rank, counts)
        ys = _moe_experts(xs, row_ids, tile_info, n_used, w_gate_up, b_gate_up, w_down, b_down, i)
        xs = _combine_ln(xs, ys, slots, gates, ln2_g, ln2_b, i)
    return xs.reshape(batch, seq, D_MODEL)
```

```python
import functools

import jax
import jax.numpy as jnp
from jax import lax
from jax.experimental import pallas as pl
from jax.experimental.pallas import tpu as pltpu

F32 = jnp.float32
BF16 = jnp.bfloat16
I32 = jnp.int32

D_MODEL = 1024
DEPTH = 4
N_MIXERS = 3
HEAD_DIM = 64
MIX_HEADS = 12
MIX_WIDTH = MIX_HEADS * HEAD_DIM
MEM_HEADS = 4
MEM_WIDTH = MEM_HEADS * HEAD_DIM
KV_HEADS = 4
KV_WIDTH = KV_HEADS * HEAD_DIM
HEADS_PER_KV = MIX_HEADS // KV_HEADS
ROT_DIM = HEAD_DIM // 4
ROPE_THETA = 500000.0
CONV_WIDTH = 3
Q_RANK = 256
IDX_HEADS = 8
IDX_DIM = 64
TOPK_MAX = 256
BLOCK = 128
N_EXPERTS = 32
TOP_K = 4
D_EXPERT = 1024
SWIGLU_LIMIT = 7.0
SWIGLU_ALPHA = 1.702
ALPHA = (2 * DEPTH) ** 0.25
LN_EPS = 1e-5

LANES = 128
QK_SCALE = HEAD_DIM ** -0.5
INT_MIN = -(2 ** 31)
NEG_BIG = -1e30
VMEM_LIMIT = 48 * 1024 * 1024

MM_TM, MM_TN = 1024, 512
PROJ_TM = 512
CONV_TS = 512
MEM_TQ = 512
LN_TM = 512
ROUTER_TM = 512
MOE_TM = 256
COMB_TC = 128
SWA_TQ = 128
DSA_TQ = 256
DSA_KC = 512


def _cparams(*sem):
    return pltpu.CompilerParams(dimension_semantics=sem, vmem_limit_bytes=VMEM_LIMIT)


def _dot(a, b):
    return jnp.dot(a, b, preferred_element_type=F32)


def _dot_nt(a, b):
    return lax.dot_general(a, b, (((1,), (1,)), ((), ())), preferred_element_type=F32)


def _layer_norm(h, g, b):
    mu = jnp.mean(h, axis=-1, keepdims=True)
    d = h - mu
    var = jnp.mean(d * d, axis=-1, keepdims=True)
    return d * lax.rsqrt(var + LN_EPS) * g + b


def _rope(x, c, s1, s2):
    outs = []
    for k in range(x.shape[1] // LANES):
        xs = x[:, k * LANES:(k + 1) * LANES]
        outs.append(xs * c + pltpu.roll(xs, 8, 1) * s1 + pltpu.roll(xs, LANES - 8, 1) * s2)
    return outs[0] if len(outs) == 1 else jnp.concatenate(outs, axis=1)


def _matmul_kernel(x_ref, w_ref, o_ref):
    o_ref[...] = _dot(x_ref[...].astype(BF16), w_ref[...].astype(BF16)).astype(o_ref.dtype)


def _matmul(x, w, layer, out_dtype, tm, tn):
    m, k = x.shape
    n = w.shape[-1]
    return pl.pallas_call(
        _matmul_kernel,
        grid=(m // tm, n // tn),
        in_specs=[pl.BlockSpec((tm, k), lambda i, j: (i, 0)),
                  pl.BlockSpec((None, k, tn), lambda i, j: (layer, 0, j))],
        out_specs=pl.BlockSpec((tm, tn), lambda i, j: (i, j)),
        out_shape=jax.ShapeDtypeStruct((m, n), out_dtype),
        compiler_params=_cparams("parallel", "parallel"),
        name="proj_matmul",
    )(x, w)


def _conv_kernel(x_ref, w_ref, cw_ref, mix_ref, qm_ref, zz_ref, *, tiles_per_seq):
    ts = x_ref.shape[0]
    wb = MIX_WIDTH

    @pl.when(pl.program_id(0) % tiles_per_seq == 0)
    def _():
        zz_ref[0:8, :] = jnp.zeros((8, wb), F32)

    proj = _dot(x_ref[...].astype(BF16), w_ref[...])
    z = proj[:, 2 * wb:3 * wb] * proj[:, 0:wb]
    zz_ref[8:, :] = z
    cw = cw_ref[...]
    conv = cw[0:1, :] * zz_ref[6:6 + ts, :] + cw[1:2, :] * zz_ref[7:7 + ts, :] + cw[2:3, :] * z
    mix_ref[...] = (proj[:, wb:2 * wb] * conv).astype(mix_ref.dtype)
    qm_ref[...] = (proj[:, 3 * wb:] * QK_SCALE).astype(qm_ref.dtype)
    zz_ref[0:8, :] = zz_ref[ts:ts + 8, :]


def _conv_mixer(x, w_in_bf, conv_w, layer, seq):
    t = x.shape[0]
    ts = CONV_TS
    n_in = w_in_bf.shape[-1]
    return pl.pallas_call(
        functools.partial(_conv_kernel, tiles_per_seq=seq // ts),
        grid=(t // ts,),
        in_specs=[pl.BlockSpec((ts, D_MODEL), lambda i: (i, 0)),
                  pl.BlockSpec((None, D_MODEL, n_in), lambda i: (layer, 0, 0)),
                  pl.BlockSpec((None, CONV_WIDTH, MIX_WIDTH), lambda i: (layer, 0, 0))],
        out_specs=[pl.BlockSpec((ts, MIX_WIDTH), lambda i: (i, 0)), pl.BlockSpec((ts, MEM_WIDTH), lambda i: (i, 0))],
        out_shape=[jax.ShapeDtypeStruct((t, MIX_WIDTH), BF16), jax.ShapeDtypeStruct((t, MEM_WIDTH), BF16)],
        scratch_shapes=[pltpu.VMEM((ts + 8, MIX_WIDTH), F32)],
        compiler_params=_cparams("arbitrary"),
        name="conv_mixer",
    )(x, w_in_bf, conv_w)


def _swa_proj_kernel(x_ref, w_ref, tab_ref, q_out, k_out, v_out, qm_out):
    tab = tab_ref[...]
    c, s1, s2 = tab[0], tab[1], tab[2]
    proj = _dot(x_ref[...].astype(BF16), w_ref[...])
    o1, o2, o3 = MIX_WIDTH, MIX_WIDTH + KV_WIDTH, MIX_WIDTH + 2 * KV_WIDTH
    q_out[...] = (_rope(proj[:, :o1], c, s1, s2) * QK_SCALE).astype(BF16)
    k_out[...] = _rope(proj[:, o1:o2], c, s1, s2).astype(BF16)
    v_out[...] = proj[:, o2:o3].astype(BF16)
    qm_out[...] = (proj[:, o3:] * QK_SCALE).astype(BF16)


def _swa_proj(x, w_in_bf, layer, rope_tab, seq):
    t = x.shape[0]
    tm = PROJ_TM
    nts = seq // tm
    n_in = w_in_bf.shape[-1]
    row = lambda w: pl.BlockSpec((tm, w), lambda i: (i, 0))
    return pl.pallas_call(
        _swa_proj_kernel,
        grid=(t // tm,),
        in_specs=[row(D_MODEL),
                  pl.BlockSpec((None, D_MODEL, n_in), lambda i: (layer, 0, 0)),
                  pl.BlockSpec((3, tm, LANES), lambda i: (0, i % nts, 0))],
        out_specs=[row(MIX_WIDTH), row(KV_WIDTH), row(KV_WIDTH), row(MEM_WIDTH)],
        out_shape=[jax.ShapeDtypeStruct((t, MIX_WIDTH), BF16), jax.ShapeDtypeStruct((t, KV_WIDTH), BF16),
                   jax.ShapeDtypeStruct((t, KV_WIDTH), BF16), jax.ShapeDtypeStruct((t, MEM_WIDTH), BF16)],
        compiler_params=_cparams("parallel"),
        name="swa_proj",
    )(x, w_in_bf, rope_tab)


def _swa_kernel(q_ref, kc_ref, kp_ref, vc_ref, vp_ref, sink_ref, o_ref, *, tiles_per_seq):
    tq = q_ref.shape[0]
    first = (pl.program_id(0) % tiles_per_seq) == 0
    q = q_ref[...]
    k = jnp.concatenate([kp_ref[...], kc_ref[...]], axis=0)
    v = jnp.concatenate([vp_ref[...], vc_ref[...]], axis=0)
    qi = lax.broadcasted_iota(I32, (tq, BLOCK + tq), 0)
    r = lax.broadcasted_iota(I32, (tq, BLOCK + tq), 1)
    keep = jnp.where(r > qi, jnp.where(r <= qi + BLOCK, 1, 0), 0)
    keep = jnp.where(first, jnp.where(r >= BLOCK, keep, 0), keep)
    bias = jnp.where(keep > 0, 0.0, -jnp.inf)
    bias = jnp.concatenate([bias] * HEADS_PER_KV, axis=0)
    sinks = sink_ref[...]
    for g in range(KV_HEADS):
        heads = range(g * HEADS_PER_KV, (g + 1) * HEADS_PER_KV)
        kv_cols = slice(g * HEAD_DIM, (g + 1) * HEAD_DIM)
        qg = jnp.concatenate([q[:, h * HEAD_DIM:(h + 1) * HEAD_DIM] for h in heads], axis=0)
        sink = jnp.concatenate([jnp.broadcast_to(sinks[0:1, h:h + 1], (tq, 1)) for h in heads], axis=0)
        logits = _dot_nt(qg, k[:, kv_cols]) + bias
        m = jnp.maximum(jnp.max(logits, axis=-1, keepdims=True), sink)
        e = jnp.exp(logits - m)
        p = e / (jnp.sum(e, axis=-1, keepdims=True) + jnp.exp(sink - m))
        out = _dot(p.astype(BF16), v[:, kv_cols])
        for n, h in enumerate(heads):
            o_ref[:, h * HEAD_DIM:(h + 1) * HEAD_DIM] = out[n * tq:(n + 1) * tq].astype(o_ref.dtype)


def _swa_mixer(q, k, v, sinks, layer, seq):
    t = q.shape[0]
    tq = SWA_TQ
    per_tile = tq // BLOCK
    prev_blk = lambda i: jnp.maximum(i * per_tile - 1, 0)
    return pl.pallas_call(
        functools.partial(_swa_kernel, tiles_per_seq=seq // tq),
        grid=(t // tq,),
        in_specs=[pl.BlockSpec((tq, MIX_WIDTH), lambda i: (i, 0)),
                  pl.BlockSpec((tq, KV_WIDTH), lambda i: (i, 0)),
                  pl.BlockSpec((BLOCK, KV_WIDTH), lambda i: (prev_blk(i), 0)),
                  pl.BlockSpec((tq, KV_WIDTH), lambda i: (i, 0)),
                  pl.BlockSpec((BLOCK, KV_WIDTH), lambda i: (prev_blk(i), 0)),
                  pl.BlockSpec((None, 1, MIX_HEADS), lambda i: (layer, 0, 0))],
        out_specs=pl.BlockSpec((tq, MIX_WIDTH), lambda i: (i, 0)),
        out_shape=jax.ShapeDtypeStruct((t, MIX_WIDTH), BF16),
        compiler_params=_cparams("parallel"),
        name="swa_mixer",
    )(q, k, k, v, v, sinks.reshape(sinks.shape[0], 1, MIX_HEADS))


def _dsa_proj_kernel(x_ref, w_ref, g_ref, wq_ref, wqi_ref, tab_ref,
                     q_out, qi_out, k_out, v_out, ki_out, wi_out, qm_out):
    tab = tab_ref[...]
    c, s1, s2 = tab[0], tab[1], tab[2]
    proj = _dot(x_ref[...].astype(BF16), w_ref[...])
    o1, o2, o3, o4 = Q_RANK, Q_RANK + KV_WIDTH, Q_RANK + 2 * KV_WIDTH, Q_RANK + 2 * KV_WIDTH + MEM_WIDTH
    cq = proj[:, :o1]
    cqn = cq * lax.rsqrt(jnp.mean(cq * cq, axis=-1, keepdims=True) + LN_EPS) * g_ref[...]
    cqn = cqn.astype(BF16)
    q_out[...] = (_rope(_dot(cqn, wq_ref[...]), c, s1, s2) * QK_SCALE).astype(BF16)
    qi_out[...] = (_rope(_dot(cqn, wqi_ref[...]), c, s1, s2) * (IDX_DIM ** -0.5)).astype(BF16)
    k_out[...] = _rope(proj[:, o1:o2], c, s1, s2).astype(BF16)
    v_out[...] = proj[:, o2:o3].astype(BF16)
    qm_out[...] = (proj[:, o3:o4] * QK_SCALE).astype(BF16)
    kiw = proj[:, o4:]
    ki_out[...] = _rope(kiw, c, s1, s2).astype(BF16)
    wi_out[...] = kiw


def _dsa_proj(x, w_in_bf, q_norm_g, w_uq_bf, w_uq_idx_bf, layer, rope_tab, seq):
    t = x.shape[0]
    tm = PROJ_TM
    nts = seq // tm
    qi_w = IDX_HEADS * IDX_DIM
    n_in = w_in_bf.shape[-1]
    row = lambda w: pl.BlockSpec((tm, w), lambda i: (i, 0))
    return pl.pallas_call(
        _dsa_proj_kernel,
        grid=(t // tm,),
        in_specs=[row(D_MODEL),
                  pl.BlockSpec((None, D_MODEL, n_in), lambda i: (layer, 0, 0)),
                  pl.BlockSpec((None, 1, Q_RANK), lambda i: (layer, 0, 0)),
                  pl.BlockSpec((None, Q_RANK, MIX_WIDTH), lambda i: (layer, 0, 0)),
                  pl.BlockSpec((None, Q_RANK, qi_w), lambda i: (layer, 0, 0)),
                  pl.BlockSpec((3, tm, LANES), lambda i: (0, i % nts, 0))],
        out_specs=[row(MIX_WIDTH), row(qi_w), row(KV_WIDTH), row(KV_WIDTH), row(LANES), row(LANES), row(MEM_WIDTH)],
        out_shape=[jax.ShapeDtypeStruct((t, MIX_WIDTH), BF16), jax.ShapeDtypeStruct((t, qi_w), BF16),
                   jax.ShapeDtypeStruct((t, KV_WIDTH), BF16), jax.ShapeDtypeStruct((t, KV_WIDTH), BF16),
                   jax.ShapeDtypeStruct((t, LANES), BF16), jax.ShapeDtypeStruct((t, LANES), F32),
                   jax.ShapeDtypeStruct((t, MEM_WIDTH), BF16)],
        compiler_params=_cparams("parallel"),
        name="dsa_proj",
    )(x, w_in_bf, q_norm_g.reshape(q_norm_g.shape[0], 1, Q_RANK), w_uq_bf, w_uq_idx_bf, rope_tab)


def _lane_fold(x, op, acc):
    for t in range(x.shape[1] // LANES):
        acc = op(acc, x[:, t * LANES:(t + 1) * LANES])
    return acc


def _loop_by_pairs(n, body, init):
    carry = lax.fori_loop(0, n // 2, lambda i, c: body(2 * i + 1, body(2 * i, c)), init)
    return lax.cond(n % 2 == 1, lambda c: body(n - 1, c), lambda c: c, carry)


def _key_to_float(key):
    return lax.bitcast_convert_type(key ^ (lax.shift_right_arithmetic(key, 31) & 0x7FFFFFFF), F32)


KEY_NEG_INF = INT_MIN + 2 ** 23 - 1


def _dsa_kernel(q_ref, qi_ref, wi_ref, k_ref, v_ref, ki_ref, o_ref, score_ref, score_t_ref, s_ref, m_ref, l_ref,
                acc_ref, *, tiles_per_seq, n_sel):
    tq = q_ref.shape[0]
    seq = k_ref.shape[0]
    kc = min(DSA_KC, seq)
    jt = pl.program_id(0) % tiles_per_seq
    n_chunks = (jt * tq + tq + kc - 1) // kc
    qpos = jt * tq + lax.broadcasted_iota(I32, (tq, 1), 0)

    def chunk_start(c):
        return pl.multiple_of(c * kc, kc)

    def key_positions(c):
        return c * kc + lax.broadcasted_iota(I32, (1, kc), 1)

    wi = wi_ref[...][:, IDX_DIM:IDX_DIM + IDX_HEADS] * (IDX_HEADS ** -0.5)
    qi = qi_ref[...]
    qi_heads = [qi[:, h * IDX_DIM:(h + 1) * IDX_DIM] for h in range(IDX_HEADS)]
    wi_heads = [wi[:, h:h + 1] for h in range(IDX_HEADS)]

    def score_chunk(c, carry):
        off = chunk_start(c)
        kch = ki_ref[pl.ds(off, kc), 0:IDX_DIM]
        score = jnp.zeros((tq, kc), F32)
        for h in range(IDX_HEADS):
            score = score + wi_heads[h] * jnp.maximum(_dot_nt(qi_heads[h], kch), 0.0)
        score = jnp.where(key_positions(c) <= qpos, score, -jnp.inf)
        score_ref[:, pl.ds(off, kc)] = score
        score_t_ref[pl.ds(off, kc), :] = score.T
        return carry

    lax.fori_loop(0, n_chunks, score_chunk, 0)

    def key_rows(c):
        return c * kc + lax.broadcasted_iota(I32, (kc, 1), 0)

    def count(pred):
        def body(c, acc):
            ones = jnp.where(pred(score_t_ref[pl.ds(chunk_start(c), kc), :], c), 1.0, 0.0)
            parts = [acc, None, None, None]
            for r in range(kc // 8):
                row = ones[r * 8:(r + 1) * 8, :]
                parts[r % 4] = row if parts[r % 4] is None else parts[r % 4] + row
            return (parts[0] + parts[1]) + (parts[2] + parts[3])
        acc = _loop_by_pairs(n_chunks, body, jnp.zeros((8, tq), F32))
        return jnp.sum(acc, axis=0, keepdims=True)

    def value_bit(it, lo):
        cand = lo + lax.shift_left(jnp.int32(1), 31 - it)
        cand_f = _key_to_float(cand)
        cnt = count(lambda sc, c: sc >= cand_f)
        take = jnp.logical_or(cnt >= n_sel, cand < KEY_NEG_INF)
        return jnp.where(take, cand, lo)

    lo = lax.fori_loop(0, 32, value_bit, jnp.full((1, tq), INT_MIN, I32))
    thr_row = _key_to_float(jnp.maximum(lo, KEY_NEG_INF))
    need = n_sel - count(lambda sc, c: sc > thr_row)
    n_eq = count(lambda sc, c: sc == thr_row)
    surplus = jnp.where(n_eq > need, jnp.where(lo > KEY_NEG_INF, 1, 0), 0)

    def tie_cut():
        idx_bits = max(seq - 1, 1).bit_length()

        def index_bit(it, cut):
            cand = cut + lax.shift_left(jnp.int32(1), idx_bits - 1 - it)
            cnt = count(lambda sc, c: jnp.logical_and(sc == thr_row, key_rows(c) < cand))
            return jnp.where(cnt < need, cand, cut)

        cut = lax.fori_loop(0, idx_bits, index_bit, jnp.zeros((1, tq), I32))
        return jnp.where(surplus > 0, cut, seq)

    cut_row = lax.cond(jnp.max(surplus) > 0, tie_cut, lambda: jnp.full((1, tq), seq, I32))

    def to_column(row):
        return jnp.broadcast_to(row, (LANES, tq)).T[:, 0:1]

    thr = to_column(thr_row)
    cut = to_column(cut_row.astype(F32)).astype(I32)

    def bias_chunk(c, carry):
        off = chunk_start(c)
        sc = score_ref[:, pl.ds(off, kc)]
        kpos = key_positions(c)
        tie = jnp.logical_and(sc == thr, kpos <= cut)
        sel = jnp.logical_and(kpos <= qpos, jnp.logical_or(sc > thr, tie))
        score_ref[:, pl.ds(off, kc)] = jnp.where(sel, 0.0, NEG_BIG)
        return carry

    lax.fori_loop(0, n_chunks, bias_chunk, 0)

    q = q_ref[...]
    for g in range(KV_HEADS):
        heads = range(g * HEADS_PER_KV, (g + 1) * HEADS_PER_KV)
        qg = jnp.concatenate([q[:, h * HEAD_DIM:(h + 1) * HEAD_DIM] for h in heads], axis=0)
        kv_cols = slice(g * HEAD_DIM, (g + 1) * HEAD_DIM)
        m_ref[...] = jnp.full(m_ref.shape, NEG_BIG, F32)
        l_ref[...] = jnp.zeros(l_ref.shape, F32)
        acc_ref[...] = jnp.zeros(acc_ref.shape, F32)

        def logits_chunk(c, carry, qg=qg, kv_cols=kv_cols):
            off = chunk_start(c)
            bias = score_ref[:, pl.ds(off, kc)]
            s = _dot_nt(qg, k_ref[pl.ds(off, kc), kv_cols]) + jnp.concatenate([bias] * HEADS_PER_KV, axis=0)
            s_ref[:, pl.ds(off, kc)] = s
            m_ref[...] = _lane_fold(s, jnp.maximum, m_ref[...])
            return carry

        _loop_by_pairs(n_chunks, logits_chunk, 0)
        m = jnp.max(m_ref[...], axis=1, keepdims=True)

        def pv_chunk(c, carry, m=m, kv_cols=kv_cols):
            off = chunk_start(c)
            p = jnp.exp(s_ref[:, pl.ds(off, kc)] - m)
            l_ref[...] = _lane_fold(p, jnp.add, l_ref[...])
            acc_ref[...] += _dot(p.astype(BF16), v_ref[pl.ds(off, kc), kv_cols])
            return carry

        _loop_by_pairs(n_chunks, pv_chunk, 0)
        out = acc_ref[...] / jnp.sum(l_ref[...], axis=1, keepdims=True)
        for r, h in enumerate(heads):
            o_ref[:, h * HEAD_DIM:(h + 1) * HEAD_DIM] = out[r * tq:(r + 1) * tq].astype(o_ref.dtype)


def _dsa_attention(q, qi, wi, k, v, ki, seq):
    t = q.shape[0]
    tq = DSA_TQ
    nts = seq // tq
    n_sel = min(TOPK_MAX, seq // 4)
    per_seq = lambda w: pl.BlockSpec((seq, w), lambda i: (i // nts, 0))
    return pl.pallas_call(
        functools.partial(_dsa_kernel, tiles_per_seq=nts, n_sel=n_sel),
        grid=(t // tq,),
        in_specs=[pl.BlockSpec((tq, MIX_WIDTH), lambda i: (i, 0)),
                  pl.BlockSpec((tq, IDX_HEADS * IDX_DIM), lambda i: (i, 0)),
                  pl.BlockSpec((tq, LANES), lambda i: (i, 0)),
                  per_seq(KV_WIDTH), per_seq(KV_WIDTH), per_seq(LANES)],
        out_specs=pl.BlockSpec((tq, MIX_WIDTH), lambda i: (i, 0)),
        out_shape=jax.ShapeDtypeStruct((t, MIX_WIDTH), BF16),
        scratch_shapes=[pltpu.VMEM((tq, seq), F32), pltpu.VMEM((seq, tq), F32),
                        pltpu.VMEM((HEADS_PER_KV * tq, seq), F32),
                        pltpu.VMEM((HEADS_PER_KV * tq, LANES), F32),
                        pltpu.VMEM((HEADS_PER_KV * tq, LANES), F32),
                        pltpu.VMEM((HEADS_PER_KV * tq, HEAD_DIM), F32)],
        compiler_params=_cparams("parallel"),
        name="dsa_attention",
    )(q, qi, wi, k, v, ki)


def _mem_attn_kernel(q_ref, k_ref, v_ref, o_ref):
    q = q_ref[...]
    for h in range(MEM_HEADS):
        sl = slice(h * HEAD_DIM, (h + 1) * HEAD_DIM)
        logits = _dot_nt(q[:, sl], k_ref[:, sl])
        e = jnp.exp(logits - jnp.max(logits, axis=-1, keepdims=True))
        p = e / jnp.sum(e, axis=-1, keepdims=True)
        o_ref[:, sl] = _dot(p.astype(BF16), v_ref[:, sl]).astype(o_ref.dtype)


def _mem_attention(qm, kv_mem, seq, mem_len):
    t = qm.shape[0]
    tq = MEM_TQ
    nqs = seq // tq
    return pl.pallas_call(
        _mem_attn_kernel,
        grid=(t // tq,),
        in_specs=[pl.BlockSpec((tq, MEM_WIDTH), lambda i: (i, 0)),
                  pl.BlockSpec((mem_len, MEM_WIDTH), lambda i: (i // nqs, 0)),
                  pl.BlockSpec((mem_len, MEM_WIDTH), lambda i: (i // nqs, 1))],
        out_specs=pl.BlockSpec((tq, MEM_WIDTH), lambda i: (i, 0)),
        out_shape=jax.ShapeDtypeStruct((t, MEM_WIDTH), BF16),
        compiler_params=_cparams("parallel"),
        name="mem_attention",
    )(qm, kv_mem, kv_mem)


def _outproj_ln_kernel(x_ref, mix_ref, mo_ref, w_ref, g_ref, b_ref, o_ref, wbf_ref):
    @pl.when(pl.program_id(0) == 0)
    def _():
        wbf_ref[...] = w_ref[...].astype(BF16)

    y = _dot(mix_ref[...], wbf_ref[0:MIX_WIDTH, :]) + _dot(mo_ref[...], wbf_ref[MIX_WIDTH:, :])
    o_ref[...] = _layer_norm(ALPHA * x_ref[...] + y, g_ref[...], b_ref[...])


def _outproj_ln(x, mix, mem_out, w_out, ln_g, ln_b, layer):
    t = x.shape[0]
    tm = LN_TM
    vec = lambda: pl.BlockSpec((None, 1, D_MODEL), lambda i: (layer, 0, 0))
    return pl.pallas_call(
        _outproj_ln_kernel,
        grid=(t // tm,),
        in_specs=[pl.BlockSpec((tm, D_MODEL), lambda i: (i, 0)),
                  pl.BlockSpec((tm, MIX_WIDTH), lambda i: (i, 0)),
                  pl.BlockSpec((tm, MEM_WIDTH), lambda i: (i, 0)),
                  pl.BlockSpec((None, D_MODEL, D_MODEL), lambda i: (layer, 0, 0)),
                  vec(), vec()],
        out_specs=pl.BlockSpec((tm, D_MODEL), lambda i: (i, 0)),
        out_shape=jax.ShapeDtypeStruct((t, D_MODEL), F32),
        scratch_shapes=[pltpu.VMEM((D_MODEL, D_MODEL), BF16)],
        compiler_params=_cparams("arbitrary"),
        name="outproj_ln",
    )(x, mix, mem_out, w_out, ln_g.reshape(DEPTH, 1, D_MODEL), ln_b.reshape(DEPTH, 1, D_MODEL))


def _router_kernel(x_ref, w_ref, b_ref, tri_ref, idx_ref, gate_ref, rank_ref, count_ref, seen_ref):
    @pl.when(pl.program_id(0) == 0)
    def _():
        seen_ref[...] = jnp.zeros(seen_ref.shape, F32)

    x = x_ref[...]
    w = w_ref[...]
    x_hi = x.astype(BF16)
    x_lo = (x - x_hi.astype(F32)).astype(BF16)
    w_hi = w.astype(BF16)
    w_lo = (w - w_hi.astype(F32)).astype(BF16)
    hi = _dot(x_hi, jnp.concatenate([w_hi, w_lo], axis=1))
    logits = (hi[:, :N_EXPERTS] + (hi[:, N_EXPERTS:] + _dot(x_lo, w_hi))) + b_ref[...]
    lane = lax.broadcasted_iota(I32, logits.shape, 1).astype(F32)
    vals, idxs = [], []
    for _ in range(TOP_K):
        m = jnp.max(logits, axis=-1, keepdims=True)
        ix = jnp.min(jnp.where(logits == m, lane, float(N_EXPERTS)), axis=-1, keepdims=True)
        vals.append(m)
        idxs.append(ix)
        logits = jnp.where(lane == ix, -jnp.inf, logits)
    e = jnp.exp(jnp.concatenate(vals, axis=1) - vals[0])
    idx_ref[...] = jnp.concatenate(idxs, axis=1).astype(I32)
    gate_ref[...] = e / jnp.sum(e, axis=-1, keepdims=True)
    hits = [jnp.where(lane == ix, 1.0, 0.0) for ix in idxs]
    routed = (hits[0] + hits[1]) + (hits[2] + hits[3])
    before = _dot(tri_ref[...], routed.astype(BF16)) + seen_ref[...]
    rank_ref[...] = jnp.concatenate([jnp.sum(hit * before, axis=-1, keepdims=True) for hit in hits],
                                    axis=1).astype(I32)
    seen_ref[...] += jnp.sum(routed, axis=0, keepdims=True)
    count_ref[...] = seen_ref[...].astype(I32)


def _router(x, router_w, router_b, layer):
    t = x.shape[0]
    tm = ROUTER_TM
    tri = jnp.tri(tm, k=-1, dtype=BF16)
    per_token = lambda: pl.BlockSpec((tm, TOP_K), lambda i: (i, 0))
    return pl.pallas_call(
        _router_kernel,
        grid=(t // tm,),
        in_specs=[pl.BlockSpec((tm, D_MODEL), lambda i: (i, 0)),
                  pl.BlockSpec((None, D_MODEL, N_EXPERTS), lambda i: (layer, 0, 0)),
                  pl.BlockSpec((None, 1, N_EXPERTS), lambda i: (layer, 0, 0)),
                  pl.BlockSpec((tm, tm), lambda i: (0, 0))],
        out_specs=[per_token(), per_token(), per_token(), pl.BlockSpec((1, N_EXPERTS), lambda i: (0, 0))],
        out_shape=[jax.ShapeDtypeStruct((t, TOP_K), I32), jax.ShapeDtypeStruct((t, TOP_K), F32),
                   jax.ShapeDtypeStruct((t, TOP_K), I32), jax.ShapeDtypeStruct((1, N_EXPERTS), I32)],
        scratch_shapes=[pltpu.VMEM((1, N_EXPERTS), F32)],
        compiler_params=_cparams("arbitrary"),
        name="router",
    )(x, router_w, router_b.reshape(DEPTH, 1, N_EXPERTS), tri)


def _dispatch_plan(top_idx, rank, counts):
    t = top_idx.shape[0]
    n_assign = t * TOP_K
    tm = MOE_TM
    n_tiles = n_assign // tm + N_EXPERTS + 1
    e = top_idx.reshape(n_assign)
    experts = jnp.arange(N_EXPERTS, dtype=I32)
    counts = counts.reshape(N_EXPERTS)
    padded = ((counts + tm - 1) // tm) * tm
    group_end = jnp.cumsum(padded)
    group_start = group_end - padded
    slot = rank.reshape(n_assign) + jnp.sum(jnp.where(e[:, None] == experts[None, :], group_start[None, :], 0), axis=1)
    n_used = group_end[-1] // tm
    tile = jnp.arange(n_tiles, dtype=I32)
    tile_expert = jnp.minimum(jnp.sum((tile[:, None] * tm >= group_end[None, :]).astype(I32), axis=1), N_EXPERTS - 1)
    last_expert = jnp.sum(jnp.where(tile == n_used - 1, tile_expert, 0))
    tile_expert = jnp.where(tile < n_used, tile_expert, last_expert).astype(I32)
    stride = 2 * n_assign
    pad = jnp.arange(tm, dtype=I32)
    pad_keys = jnp.where(pad[None, :] < (padded - counts)[:, None],
                         experts[:, None] * stride + n_assign + pad[None, :], jnp.iinfo(I32).max)
    keys = jnp.sort(jnp.concatenate([e * stride + jnp.arange(n_assign, dtype=I32), pad_keys.reshape(-1)]))
    index = keys % stride
    row_ids = jnp.where(index < n_assign, index // TOP_K, 0)
    row_ids = jnp.concatenate([row_ids, jnp.zeros(((n_tiles - n_assign // tm - N_EXPERTS) * tm,), I32)])
    nonempty = counts > 0
    later = jnp.where(nonempty[None, :] & (experts[None, :] > experts[:, None]), experts[None, :], N_EXPERTS)
    next_expert = jnp.min(later, axis=1)
    next_expert = jnp.where(next_expert < N_EXPERTS, next_expert, -1)
    group_parity = (jnp.cumsum(nonempty.astype(I32)) - 1) % 2
    info = jnp.stack([tile_expert, next_expert[tile_expert], group_parity[tile_expert]]).astype(I32)
    return row_ids.reshape(n_tiles, 1, tm), info, n_used.reshape(1).astype(I32), slot.reshape(t, TOP_K)


def _gather_rows(src_hbm, ids_ref, n_rows, dst_ref, sem):
    def body(r, carry):
        pltpu.make_async_copy(src_hbm.at[pl.ds(ids_ref[0, 0, r], 1)], dst_ref.at[pl.ds(r, 1)], sem).start()
        return carry
    lax.fori_loop(0, n_rows, body, 0, unroll=8)


def _wait_rows(dst_ref, sem):
    pltpu.make_async_copy(dst_ref, dst_ref, sem).wait()


def _moe_kernel(info_ref, nu_ref, ids_ref, ids_1_ref, ids_2_ref, x_hbm, wgu_hbm, bgu_ref, wd_hbm, bd_ref, o_ref,
                xbuf0, xbuf1, xbuf2, sems, wgu_f32, wd_f32, wsems, wgu_bf, wd_bf, *, layer):
    j = pl.program_id(0)
    n_used = nu_ref[0]
    tm = xbuf0.shape[0]
    bufs = ((xbuf0, sems.at[0]), (xbuf1, sems.at[1]), (xbuf2, sems.at[2]))
    expert, next_expert, wslot = info_ref[0, j], info_ref[1, j], info_ref[2, j]

    def weight_copies(e, slot):
        return (pltpu.make_async_copy(wgu_hbm.at[layer, e], wgu_f32.at[slot], wsems.at[0, slot]),
                pltpu.make_async_copy(wd_hbm.at[layer, e], wd_f32.at[slot], wsems.at[1, slot]))

    @pl.when(j == 0)
    def _():
        for cp in weight_copies(expert, wslot):
            cp.start()
        _gather_rows(x_hbm, ids_ref, tm, xbuf0, sems.at[0])
        _gather_rows(x_hbm, ids_1_ref, tm, xbuf1, sems.at[1])

    def expert_tile(cur, nxt):
        _wait_rows(*cur)

        @pl.when(jnp.logical_or(j == 0, expert != info_ref[0, jnp.maximum(j - 1, 0)]))
        def _():
            for cp in weight_copies(expert, wslot):
                cp.wait()

            @pl.when(next_expert >= 0)
            def _():
                for cp in weight_copies(next_expert, 1 - wslot):
                    cp.start()

            wgu_bf[...] = wgu_f32[wslot].astype(BF16)
            wd_bf[...] = wd_f32[wslot].astype(BF16)

        for r in range(tm):
            pltpu.make_async_copy(x_hbm.at[pl.ds(ids_2_ref[0, 0, r], 1)], nxt[0].at[pl.ds(r, 1)],
                                  nxt[1]).start(priority=r % 2)
        h = _dot(cur[0][...].astype(BF16), wgu_bf[...]) + bgu_ref[...]
        gate = jnp.minimum(h[:, :D_EXPERT], SWIGLU_LIMIT)
        up = jnp.clip(h[:, D_EXPERT:], -SWIGLU_LIMIT, SWIGLU_LIMIT)
        act = (up + 1.0) * gate * jax.nn.sigmoid(SWIGLU_ALPHA * gate)
        o_ref[...] = _dot(act.astype(BF16), wd_bf[...]) + bd_ref[...]

    for m in range(3):
        @pl.when(jnp.logical_and(j < n_used, j % 3 == m))
        def _(m=m):
            expert_tile(bufs[m], bufs[(m + 2) % 3])

        @pl.when(jnp.logical_and(jnp.logical_and(j >= n_used, j <= n_used + 1), j % 3 == m))
        def _(m=m):
            _wait_rows(*bufs[m])

    @pl.when(j >= n_used)
    def _():
        o_ref[...] = jnp.zeros_like(o_ref)


def _moe_experts(x, row_ids, tile_info, n_used, w_gate_up, b_gate_up, w_down, b_down, layer):
    n_tiles = row_ids.shape[0]
    tm = MOE_TM
    grid_spec = pltpu.PrefetchScalarGridSpec(
        num_scalar_prefetch=2,
        grid=(n_tiles,),
        in_specs=[pl.BlockSpec((1, 1, tm), lambda j, info, nu: (j, 0, 0), memory_space=pltpu.SMEM),
                  pl.BlockSpec((1, 1, tm), lambda j, info, nu: (jnp.minimum(j + 1, n_tiles - 1), 0, 0),
                               memory_space=pltpu.SMEM),
                  pl.BlockSpec((1, 1, tm), lambda j, info, nu: (jnp.minimum(j + 2, n_tiles - 1), 0, 0),
                               memory_space=pltpu.SMEM),
                  pl.BlockSpec(memory_space=pl.ANY),
                  pl.BlockSpec(memory_space=pl.ANY),
                  pl.BlockSpec((None, None, 1, 2 * D_EXPERT), lambda j, info, nu: (layer, info[0, j], 0, 0)),
                  pl.BlockSpec(memory_space=pl.ANY),
                  pl.BlockSpec((None, None, 1, D_MODEL), lambda j, info, nu: (layer, info[0, j], 0, 0))],
        out_specs=pl.BlockSpec((tm, D_MODEL), lambda j, info, nu: (j, 0)),
        scratch_shapes=[pltpu.VMEM((tm, D_MODEL), F32), pltpu.VMEM((tm, D_MODEL), F32), pltpu.VMEM((tm, D_MODEL), F32),
                        pltpu.SemaphoreType.DMA((3,)),
                        pltpu.VMEM((2, D_MODEL, 2 * D_EXPERT), F32), pltpu.VMEM((2, D_EXPERT, D_MODEL), F32),
                        pltpu.SemaphoreType.DMA((2, 2)),
                        pltpu.VMEM((D_MODEL, 2 * D_EXPERT), BF16), pltpu.VMEM((D_EXPERT, D_MODEL), BF16)],
    )
    return pl.pallas_call(
        functools.partial(_moe_kernel, layer=layer),
        grid_spec=grid_spec,
        out_shape=jax.ShapeDtypeStruct((n_tiles * tm, D_MODEL), F32),
        compiler_params=_cparams("arbitrary"),
        name="moe_experts",
    )(tile_info, n_used, row_ids, row_ids, row_ids, x, w_gate_up,
      b_gate_up.reshape(DEPTH, N_EXPERTS, 1, 2 * D_EXPERT), w_down, b_down.reshape(DEPTH, N_EXPERTS, 1, D_MODEL))


def _combine_ln_kernel(slot_ref, slot_1_ref, slot_2_ref, ys_hbm, gate_ref, x_ref, g_ref, b_ref, o_ref,
                       buf0, buf1, buf2, sems, *, n_steps):
    i = pl.program_id(0)
    rows = buf0.shape[0]
    tc = rows // TOP_K
    bufs = ((buf0, sems.at[0]), (buf1, sems.at[1]), (buf2, sems.at[2]))

    @pl.when(i == 0)
    def _():
        _gather_rows(ys_hbm, slot_ref, rows, buf0, sems.at[0])
        _gather_rows(ys_hbm, slot_1_ref, rows, buf1, sems.at[1])

    def token_tile(cur, nxt):
        _wait_rows(*cur)
        if nxt is not None:
            for r in range(rows):
                pltpu.make_async_copy(ys_hbm.at[pl.ds(slot_2_ref[0, 0, r], 1)], nxt[0].at[pl.ds(r, 1)],
                                      nxt[1]).start(priority=r % 2)
        gates = gate_ref[...]
        f = gates[:, 0:1] * cur[0][0:tc, :]
        for k in range(1, TOP_K):
            f = f + gates[:, k:k + 1] * cur[0][k * tc:(k + 1) * tc, :]
        o_ref[...] = _layer_norm(ALPHA * x_ref[...] + f, g_ref[...], b_ref[...])

    for m in range(3):
        @pl.when(jnp.logical_and(i < n_steps - 2, i % 3 == m))
        def _(m=m):
            token_tile(bufs[m], bufs[(m + 2) % 3])

    for last in (n_steps - 2, n_steps - 1):
        @pl.when(i == last)
        def _(last=last):
            token_tile(bufs[last % 3], None)


def _combine_ln(x, ys, slots, gates, ln_g, ln_b, layer):
    t = x.shape[0]
    tc = COMB_TC
    n = t // tc
    slots_km = slots.reshape(n, tc, TOP_K).transpose(0, 2, 1).reshape(n, 1, TOP_K * tc)
    vec = lambda: pl.BlockSpec((None, 1, D_MODEL), lambda i: (layer, 0, 0))
    return pl.pallas_call(
        functools.partial(_combine_ln_kernel, n_steps=n),
        grid=(n,),
        in_specs=[pl.BlockSpec((1, 1, TOP_K * tc), lambda i: (i, 0, 0), memory_space=pltpu.SMEM),
                  pl.BlockSpec((1, 1, TOP_K * tc), lambda i: (jnp.minimum(i + 1, n - 1), 0, 0),
                               memory_space=pltpu.SMEM),
                  pl.BlockSpec((1, 1, TOP_K * tc), lambda i: (jnp.minimum(i + 2, n - 1), 0, 0),
                               memory_space=pltpu.SMEM),
                  pl.BlockSpec(memory_space=pl.ANY),
                  pl.BlockSpec((tc, TOP_K), lambda i: (i, 0)),
                  pl.BlockSpec((tc, D_MODEL), lambda i: (i, 0)),
                  vec(), vec()],
        out_specs=pl.BlockSpec((tc, D_MODEL), lambda i: (i, 0)),
        out_shape=jax.ShapeDtypeStruct((t, D_MODEL), F32),
        scratch_shapes=[pltpu.VMEM((TOP_K * tc, D_MODEL), F32), pltpu.VMEM((TOP_K * tc, D_MODEL), F32),
                        pltpu.VMEM((TOP_K * tc, D_MODEL), F32), pltpu.SemaphoreType.DMA((3,))],
        compiler_params=_cparams("arbitrary"),
        name="combine_ln",
    )(slots_km, slots_km, slots_km, ys, gates, x, ln_g.reshape(DEPTH, 1, D_MODEL), ln_b.reshape(DEPTH, 1, D_MODEL))


def _rope_tables(seq):
    inv = ROPE_THETA ** (-jnp.arange(0, ROT_DIM, 2, dtype=F32) / ROT_DIM)
    ang = jnp.arange(seq, dtype=F32)[:, None] * inv[None, :]
    cos, sin = jnp.cos(ang), jnp.sin(ang)
    half = ROT_DIM // 2
    rest = HEAD_DIM - ROT_DIM
    zeros_h = jnp.zeros((seq, half), F32)
    head = lambda a, b, fill: jnp.concatenate([a, b, jnp.full((seq, rest), fill, F32)], axis=1)
    tab = jnp.stack([head(cos, cos, 1.0), head(zeros_h, sin, 0.0), head(-sin, zeros_h, 0.0)])
    return jnp.tile(tab, (1, 1, LANES // HEAD_DIM))


def kernel(x, mem, a_w_in, a_conv_w, b_w_in, b_q_norm_g, b_w_uq, b_w_uq_idx, c_w_in, c_sinks, w_out, w_kv_mem,
           ln1_g, ln1_b, router_w, router_b, w_gate_up, b_gate_up, w_down, b_down, ln2_g, ln2_b):
    batch, seq, _ = x.shape
    mem_len = mem.shape[1]
    t = batch * seq
    xs = x.reshape(t, D_MODEL)
    mem2 = mem.reshape(batch * mem_len, D_MODEL)
    rope_tab = _rope_tables(seq)

    o1 = Q_RANK
    o3 = o1 + 2 * KV_WIDTH
    o5 = o3 + IDX_DIM + IDX_HEADS
    b_w = jnp.concatenate([b_w_in[:, :, :o3], b_w_in[:, :, o5:], b_w_in[:, :, o3:o5],
                           jnp.zeros(b_w_in.shape[:2] + (LANES - IDX_DIM - IDX_HEADS,), b_w_in.dtype)], axis=2)
    a_w, b_w, c_w = a_w_in.astype(BF16), b_w.astype(BF16), c_w_in.astype(BF16)
    b_wq, b_wqi = b_w_uq.astype(BF16), b_w_uq_idx.astype(BF16)

    for i in range(DEPTH):
        kind, j = i % N_MIXERS, i // N_MIXERS
        if kind == 0:
            mix, qm = _conv_mixer(xs, a_w, a_conv_w, j, seq)
        elif kind == 1:
            q, qi, k, v, ki, wi, qm = _dsa_proj(xs, b_w, b_q_norm_g, b_wq, b_wqi, j, rope_tab, seq)
            mix = _dsa_attention(q, qi, wi, k, v, ki, seq)
        else:
            q, k, v, qm = _swa_proj(xs, c_w, j, rope_tab, seq)
            mix = _swa_mixer(q, k, v, c_sinks, j, seq)
        kv_mem = _matmul(mem2, w_kv_mem, i, BF16, min(MM_TM, batch * mem_len), MM_TN)
        mem_out = _mem_attention(qm, kv_mem, seq, mem_len)
        xs = _outproj_ln(xs, mix, mem_out, w_out, ln1_g, ln1_b, i)
        top_idx, gates, rank, counts = _router(xs, router_w, router_b, i)
        row_ids, tile_info, n_used, slots = _dispatch_plan(top_idx, rank, counts)
        ys = _moe_experts(xs, row_ids, tile_info, n_used, w_gate_up, b_gate_up, w_down, b_down, i)
        xs = _combine_ln(xs, ys, slots, gates, ln2_g, ln2_b, i)
    return xs.reshape(batch, seq, D_MODEL)
```

```python
import functools

import jax
import jax.numpy as jnp
from jax import lax
from jax.experimental import pallas as pl
from jax.experimental.pallas import tpu as pltpu

F32 = jnp.float32
BF16 = jnp.bfloat16
I32 = jnp.int32

D_MODEL = 1024
DEPTH = 4
N_MIXERS = 3
HEAD_DIM = 64
MIX_HEADS = 12
MIX_WIDTH = MIX_HEADS * HEAD_DIM
MEM_HEADS = 4
MEM_WIDTH = MEM_HEADS * HEAD_DIM
KV_HEADS = 4
KV_WIDTH = KV_HEADS * HEAD_DIM
HEADS_PER_KV = MIX_HEADS // KV_HEADS
ROT_DIM = HEAD_DIM // 4
ROPE_THETA = 500000.0
CONV_WIDTH = 3
Q_RANK = 256
IDX_HEADS = 8
IDX_DIM = 64
TOPK_MAX = 256
BLOCK = 128
N_EXPERTS = 32
TOP_K = 4
D_EXPERT = 1024
SWIGLU_LIMIT = 7.0
SWIGLU_ALPHA = 1.702
ALPHA = (2 * DEPTH) ** 0.25
LN_EPS = 1e-5

LANES = 128
SUBLANES = 8
QK_SCALE = HEAD_DIM ** -0.5
INT_MIN = -(2 ** 31)
NEG_BIG = -1e30
VMEM_LIMIT = 48 * 1024 * 1024

MM_TM, MM_TN = 1024, 512
PROJ_TM = 512
CONV_TS = 512
MEM_TQ = 512
LN_TM = 512
ROUTER_TM = 512
MOE_TM = 256
COMB_TC = 128
SWA_TQ = 128
DSA_TQ = 256
DSA_KC = 512


def _cparams(*sem):
    return pltpu.CompilerParams(dimension_semantics=sem, vmem_limit_bytes=VMEM_LIMIT)


def _dot(a, b):
    return jnp.dot(a, b, preferred_element_type=F32)


def _dot_nt(a, b):
    return lax.dot_general(a, b, (((1,), (1,)), ((), ())), preferred_element_type=F32)


def _layer_norm(h, g, b):
    mu = jnp.mean(h, axis=-1, keepdims=True)
    d = h - mu
    var = jnp.mean(d * d, axis=-1, keepdims=True)
    return d * lax.rsqrt(var + LN_EPS) * g + b


def _rope(x, c, s1, s2):
    outs = []
    for k in range(x.shape[1] // LANES):
        xs = x[:, k * LANES:(k + 1) * LANES]
        outs.append(xs * c + pltpu.roll(xs, 8, 1) * s1 + pltpu.roll(xs, LANES - 8, 1) * s2)
    return outs[0] if len(outs) == 1 else jnp.concatenate(outs, axis=1)


def _matmul_kernel(x_ref, w_ref, o_ref):
    o_ref[...] = _dot(x_ref[...].astype(BF16), w_ref[...].astype(BF16)).astype(o_ref.dtype)


def _matmul(x, w, layer, out_dtype, tm, tn):
    m, k = x.shape
    n = w.shape[-1]
    return pl.pallas_call(
        _matmul_kernel,
        grid=(m // tm, n // tn),
        in_specs=[pl.BlockSpec((tm, k), lambda i, j: (i, 0)),
                  pl.BlockSpec((None, k, tn), lambda i, j: (layer, 0, j))],
        out_specs=pl.BlockSpec((tm, tn), lambda i, j: (i, j)),
        out_shape=jax.ShapeDtypeStruct((m, n), out_dtype),
        compiler_params=_cparams("parallel", "parallel"),
        name="proj_matmul",
    )(x, w)


def _conv_kernel(x_ref, w_ref, cw_ref, mix_ref, qm_ref, zz_ref, *, tiles_per_seq):
    ts = x_ref.shape[0]
    wb = MIX_WIDTH

    @pl.when(pl.program_id(0) % tiles_per_seq == 0)
    def _():
        zz_ref[0:SUBLANES, :] = jnp.zeros((SUBLANES, wb), F32)

    proj = _dot(x_ref[...].astype(BF16), w_ref[...])
    z = proj[:, 2 * wb:3 * wb] * proj[:, 0:wb]
    zz_ref[SUBLANES:, :] = z
    cw = cw_ref[...]
    conv = (cw[0:1, :] * zz_ref[SUBLANES - 2:SUBLANES - 2 + ts, :]
            + cw[1:2, :] * zz_ref[SUBLANES - 1:SUBLANES - 1 + ts, :] + cw[2:3, :] * z)
    mix_ref[...] = (proj[:, wb:2 * wb] * conv).astype(mix_ref.dtype)
    qm_ref[...] = (proj[:, 3 * wb:] * QK_SCALE).astype(qm_ref.dtype)
    zz_ref[0:SUBLANES, :] = zz_ref[ts:ts + SUBLANES, :]


def _conv_mixer(x, w_in_bf, conv_w, layer, seq):
    t = x.shape[0]
    ts = CONV_TS
    n_in = w_in_bf.shape[-1]
    return pl.pallas_call(
        functools.partial(_conv_kernel, tiles_per_seq=seq // ts),
        grid=(t // ts,),
        in_specs=[pl.BlockSpec((ts, D_MODEL), lambda i: (i, 0)),
                  pl.BlockSpec((None, D_MODEL, n_in), lambda i: (layer, 0, 0)),
                  pl.BlockSpec((None, CONV_WIDTH, MIX_WIDTH), lambda i: (layer, 0, 0))],
        out_specs=[pl.BlockSpec((ts, MIX_WIDTH), lambda i: (i, 0)), pl.BlockSpec((ts, MEM_WIDTH), lambda i: (i, 0))],
        out_shape=[jax.ShapeDtypeStruct((t, MIX_WIDTH), BF16), jax.ShapeDtypeStruct((t, MEM_WIDTH), BF16)],
        scratch_shapes=[pltpu.VMEM((ts + SUBLANES, MIX_WIDTH), F32)],
        compiler_params=_cparams("arbitrary"),
        name="conv_mixer",
    )(x, w_in_bf, conv_w)


def _swa_proj_kernel(x_ref, w_ref, tab_ref, q_out, k_out, v_out, qm_out):
    tab = tab_ref[...]
    c, s1, s2 = tab[0], tab[1], tab[2]
    proj = _dot(x_ref[...].astype(BF16), w_ref[...])
    o1, o2, o3 = MIX_WIDTH, MIX_WIDTH + KV_WIDTH, MIX_WIDTH + 2 * KV_WIDTH
    q_out[...] = (_rope(proj[:, :o1], c, s1, s2) * QK_SCALE).astype(BF16)
    k_out[...] = _rope(proj[:, o1:o2], c, s1, s2).astype(BF16)
    v_out[...] = proj[:, o2:o3].astype(BF16)
    qm_out[...] = (proj[:, o3:] * QK_SCALE).astype(BF16)


def _swa_proj(x, w_in_bf, layer, rope_tab, seq):
    t = x.shape[0]
    tm = PROJ_TM
    nts = seq // tm
    n_in = w_in_bf.shape[-1]
    row = lambda w: pl.BlockSpec((tm, w), lambda i: (i, 0))
    return pl.pallas_call(
        _swa_proj_kernel,
        grid=(t // tm,),
        in_specs=[row(D_MODEL),
                  pl.BlockSpec((None, D_MODEL, n_in), lambda i: (layer, 0, 0)),
                  pl.BlockSpec((3, tm, LANES), lambda i: (0, i % nts, 0))],
        out_specs=[row(MIX_WIDTH), row(KV_WIDTH), row(KV_WIDTH), row(MEM_WIDTH)],
        out_shape=[jax.ShapeDtypeStruct((t, MIX_WIDTH), BF16), jax.ShapeDtypeStruct((t, KV_WIDTH), BF16),
                   jax.ShapeDtypeStruct((t, KV_WIDTH), BF16), jax.ShapeDtypeStruct((t, MEM_WIDTH), BF16)],
        compiler_params=_cparams("parallel"),
        name="swa_proj",
    )(x, w_in_bf, rope_tab)


def _swa_kernel(q_ref, kc_ref, kp_ref, vc_ref, vp_ref, sink_ref, o_ref, *, tiles_per_seq):
    tq = q_ref.shape[0]
    first = (pl.program_id(0) % tiles_per_seq) == 0
    q = q_ref[...]
    k = jnp.concatenate([kp_ref[...], kc_ref[...]], axis=0)
    v = jnp.concatenate([vp_ref[...], vc_ref[...]], axis=0)
    qi = lax.broadcasted_iota(I32, (tq, BLOCK + tq), 0)
    r = lax.broadcasted_iota(I32, (tq, BLOCK + tq), 1)
    keep = jnp.where(r > qi, jnp.where(r <= qi + BLOCK, 1, 0), 0)
    keep = jnp.where(first, jnp.where(r >= BLOCK, keep, 0), keep)
    bias = jnp.where(keep > 0, 0.0, -jnp.inf)
    bias = jnp.concatenate([bias] * HEADS_PER_KV, axis=0)
    sinks = sink_ref[...]
    for g in range(KV_HEADS):
        heads = range(g * HEADS_PER_KV, (g + 1) * HEADS_PER_KV)
        kv_cols = slice(g * HEAD_DIM, (g + 1) * HEAD_DIM)
        qg = jnp.concatenate([q[:, h * HEAD_DIM:(h + 1) * HEAD_DIM] for h in heads], axis=0)
        sink = jnp.concatenate([jnp.broadcast_to(sinks[0:1, h:h + 1], (tq, 1)) for h in heads], axis=0)
        logits = _dot_nt(qg, k[:, kv_cols]) + bias
        m = jnp.maximum(jnp.max(logits, axis=-1, keepdims=True), sink)
        e = jnp.exp(logits - m)
        p = e / (jnp.sum(e, axis=-1, keepdims=True) + jnp.exp(sink - m))
        out = _dot(p.astype(BF16), v[:, kv_cols])
        for n, h in enumerate(heads):
            o_ref[:, h * HEAD_DIM:(h + 1) * HEAD_DIM] = out[n * tq:(n + 1) * tq].astype(o_ref.dtype)


def _swa_mixer(q, k, v, sinks, layer, seq):
    t = q.shape[0]
    tq = SWA_TQ
    per_tile = tq // BLOCK
    prev_blk = lambda i: jnp.maximum(i * per_tile - 1, 0)
    return pl.pallas_call(
        functools.partial(_swa_kernel, tiles_per_seq=seq // tq),
        grid=(t // tq,),
        in_specs=[pl.BlockSpec((tq, MIX_WIDTH), lambda i: (i, 0)),
                  pl.BlockSpec((tq, KV_WIDTH), lambda i: (i, 0)),
                  pl.BlockSpec((BLOCK, KV_WIDTH), lambda i: (prev_blk(i), 0)),
                  pl.BlockSpec((tq, KV_WIDTH), lambda i: (i, 0)),
                  pl.BlockSpec((BLOCK, KV_WIDTH), lambda i: (prev_blk(i), 0)),
                  pl.BlockSpec((None, 1, MIX_HEADS), lambda i: (layer, 0, 0))],
        out_specs=pl.BlockSpec((tq, MIX_WIDTH), lambda i: (i, 0)),
        out_shape=jax.ShapeDtypeStruct((t, MIX_WIDTH), BF16),
        compiler_params=_cparams("parallel"),
        name="swa_mixer",
    )(q, k, k, v, v, sinks.reshape(sinks.shape[0], 1, MIX_HEADS))


def _dsa_proj_kernel(x_ref, w_ref, g_ref, wq_ref, wqi_ref, tab_ref,
                     q_out, qi_out, k_out, v_out, ki_out, wi_out, qm_out):
    tab = tab_ref[...]
    c, s1, s2 = tab[0], tab[1], tab[2]
    proj = _dot(x_ref[...].astype(BF16), w_ref[...])
    o1, o2, o3, o4 = Q_RANK, Q_RANK + KV_WIDTH, Q_RANK + 2 * KV_WIDTH, Q_RANK + 2 * KV_WIDTH + MEM_WIDTH
    cq = proj[:, :o1]
    cqn = cq * lax.rsqrt(jnp.mean(cq * cq, axis=-1, keepdims=True) + LN_EPS) * g_ref[...]
    cqn = cqn.astype(BF16)
    q_out[...] = (_rope(_dot(cqn, wq_ref[...]), c, s1, s2) * QK_SCALE).astype(BF16)
    qi_out[...] = (_rope(_dot(cqn, wqi_ref[...]), c, s1, s2) * (IDX_DIM ** -0.5)).astype(BF16)
    k_out[...] = _rope(proj[:, o1:o2], c, s1, s2).astype(BF16)
    v_out[...] = proj[:, o2:o3].astype(BF16)
    qm_out[...] = (proj[:, o3:o4] * QK_SCALE).astype(BF16)
    kiw = proj[:, o4:]
    ki_out[...] = _rope(kiw, c, s1, s2).astype(BF16)
    wi_out[...] = kiw


def _dsa_proj(x, w_in_bf, q_norm_g, w_uq_bf, w_uq_idx_bf, layer, rope_tab, seq):
    t = x.shape[0]
    tm = PROJ_TM
    nts = seq // tm
    qi_w = IDX_HEADS * IDX_DIM
    n_in = w_in_bf.shape[-1]
    row = lambda w: pl.BlockSpec((tm, w), lambda i: (i, 0))
    return pl.pallas_call(
        _dsa_proj_kernel,
        grid=(t // tm,),
        in_specs=[row(D_MODEL),
                  pl.BlockSpec((None, D_MODEL, n_in), lambda i: (layer, 0, 0)),
                  pl.BlockSpec((None, 1, Q_RANK), lambda i: (layer, 0, 0)),
                  pl.BlockSpec((None, Q_RANK, MIX_WIDTH), lambda i: (layer, 0, 0)),
                  pl.BlockSpec((None, Q_RANK, qi_w), lambda i: (layer, 0, 0)),
                  pl.BlockSpec((3, tm, LANES), lambda i: (0, i % nts, 0))],
        out_specs=[row(MIX_WIDTH), row(qi_w), row(KV_WIDTH), row(KV_WIDTH), row(LANES), row(LANES), row(MEM_WIDTH)],
        out_shape=[jax.ShapeDtypeStruct((t, MIX_WIDTH), BF16), jax.ShapeDtypeStruct((t, qi_w), BF16),
                   jax.ShapeDtypeStruct((t, KV_WIDTH), BF16), jax.ShapeDtypeStruct((t, KV_WIDTH), BF16),
                   jax.ShapeDtypeStruct((t, LANES), BF16), jax.ShapeDtypeStruct((t, LANES), F32),
                   jax.ShapeDtypeStruct((t, MEM_WIDTH), BF16)],
        compiler_params=_cparams("parallel"),
        name="dsa_proj",
    )(x, w_in_bf, q_norm_g.reshape(q_norm_g.shape[0], 1, Q_RANK), w_uq_bf, w_uq_idx_bf, rope_tab)


def _lane_fold(x, op, acc):
    for t in range(x.shape[1] // LANES):
        acc = op(acc, x[:, t * LANES:(t + 1) * LANES])
    return acc


def _loop_by_pairs(n, body, init):
    carry = lax.fori_loop(0, n // 2, lambda i, c: body(2 * i + 1, body(2 * i, c)), init)
    return lax.cond(n % 2 == 1, lambda c: body(n - 1, c), lambda c: c, carry)


def _key_to_float(key):
    return lax.bitcast_convert_type(key ^ (lax.shift_right_arithmetic(key, 31) & 0x7FFFFFFF), F32)


KEY_NEG_INF = INT_MIN + 2 ** 23 - 1


def _dsa_kernel(q_ref, qi_ref, wi_ref, k_ref, v_ref, ki_ref, o_ref, score_ref, score_t_ref, s_ref, m_ref, l_ref,
                acc_ref, *, tiles_per_seq, n_sel):
    tq = q_ref.shape[0]
    seq = k_ref.shape[0]
    kc = min(DSA_KC, seq)
    jt = pl.program_id(0) % tiles_per_seq
    n_chunks = (jt * tq + tq + kc - 1) // kc
    qpos = jt * tq + lax.broadcasted_iota(I32, (tq, 1), 0)

    def chunk_start(c):
        return pl.multiple_of(c * kc, kc)

    def key_positions(c):
        return c * kc + lax.broadcasted_iota(I32, (1, kc), 1)

    wi = wi_ref[...][:, IDX_DIM:IDX_DIM + IDX_HEADS] * (IDX_HEADS ** -0.5)
    qi = qi_ref[...]
    qi_heads = [qi[:, h * IDX_DIM:(h + 1) * IDX_DIM] for h in range(IDX_HEADS)]
    wi_heads = [wi[:, h:h + 1] for h in range(IDX_HEADS)]

    def score_chunk(c, carry):
        off = chunk_start(c)
        kch = ki_ref[pl.ds(off, kc), 0:IDX_DIM]
        score = jnp.zeros((tq, kc), F32)
        for h in range(IDX_HEADS):
            score = score + wi_heads[h] * jnp.maximum(_dot_nt(qi_heads[h], kch), 0.0)
        score = jnp.where(key_positions(c) <= qpos, score, -jnp.inf)
        score_ref[:, pl.ds(off, kc)] = score
        score_t_ref[pl.ds(off, kc), :] = score.T
        return carry

    lax.fori_loop(0, n_chunks, score_chunk, 0)

    def key_rows(c):
        return c * kc + lax.broadcasted_iota(I32, (kc, 1), 0)

    def count(pred):
        def body(c, acc):
            ones = jnp.where(pred(score_t_ref[pl.ds(chunk_start(c), kc), :], c), 1.0, 0.0)
            parts = [acc, None, None, None]
            for r in range(kc // SUBLANES):
                row = ones[r * SUBLANES:(r + 1) * SUBLANES, :]
                parts[r % 4] = row if parts[r % 4] is None else parts[r % 4] + row
            return (parts[0] + parts[1]) + (parts[2] + parts[3])
        acc = _loop_by_pairs(n_chunks, body, jnp.zeros((SUBLANES, tq), F32))
        return jnp.sum(acc, axis=0, keepdims=True)

    def value_bit(it, lo):
        cand = lo + lax.shift_left(jnp.int32(1), 31 - it)
        cand_f = _key_to_float(cand)
        cnt = count(lambda sc, c: sc >= cand_f)
        take = jnp.logical_or(cnt >= n_sel, cand < KEY_NEG_INF)
        return jnp.where(take, cand, lo)

    lo = lax.fori_loop(0, 32, value_bit, jnp.full((1, tq), INT_MIN, I32))
    thr_row = _key_to_float(jnp.maximum(lo, KEY_NEG_INF))
    need = n_sel - count(lambda sc, c: sc > thr_row)
    n_eq = count(lambda sc, c: sc == thr_row)
    surplus = jnp.where(n_eq > need, jnp.where(lo > KEY_NEG_INF, 1, 0), 0)

    def tie_cut():
        idx_bits = max(seq - 1, 1).bit_length()

        def index_bit(it, cut):
            cand = cut + lax.shift_left(jnp.int32(1), idx_bits - 1 - it)
            cnt = count(lambda sc, c: jnp.logical_and(sc == thr_row, key_rows(c) < cand))
            return jnp.where(cnt < need, cand, cut)

        cut = lax.fori_loop(0, idx_bits, index_bit, jnp.zeros((1, tq), I32))
        return jnp.where(surplus > 0, cut, seq)

    cut_row = lax.cond(jnp.max(surplus) > 0, tie_cut, lambda: jnp.full((1, tq), seq, I32))

    def to_column(row):
        return jnp.broadcast_to(row, (LANES, tq)).T[:, 0:1]

    thr = to_column(thr_row)
    cut = to_column(cut_row.astype(F32)).astype(I32)

    def bias_chunk(c, carry):
        off = chunk_start(c)
        sc = score_ref[:, pl.ds(off, kc)]
        kpos = key_positions(c)
        tie = jnp.logical_and(sc == thr, kpos <= cut)
        sel = jnp.logical_and(kpos <= qpos, jnp.logical_or(sc > thr, tie))
        score_ref[:, pl.ds(off, kc)] = jnp.where(sel, 0.0, NEG_BIG)
        return carry

    _loop_by_pairs(n_chunks, bias_chunk, 0)

    q = q_ref[...]
    for g in range(KV_HEADS):
        heads = range(g * HEADS_PER_KV, (g + 1) * HEADS_PER_KV)
        qg = jnp.concatenate([q[:, h * HEAD_DIM:(h + 1) * HEAD_DIM] for h in heads], axis=0)
        kv_cols = slice(g * HEAD_DIM, (g + 1) * HEAD_DIM)
        m_ref[...] = jnp.full(m_ref.shape, NEG_BIG, F32)
        l_ref[...] = jnp.zeros(l_ref.shape, F32)
        acc_ref[...] = jnp.zeros(acc_ref.shape, F32)

        def logits_chunk(c, carry, qg=qg, kv_cols=kv_cols):
            off = chunk_start(c)
            bias = score_ref[:, pl.ds(off, kc)]
            s = _dot_nt(qg, k_ref[pl.ds(off, kc), kv_cols]) + jnp.concatenate([bias] * HEADS_PER_KV, axis=0)
            s_ref[:, pl.ds(off, kc)] = s
            m_ref[...] = _lane_fold(s, jnp.maximum, m_ref[...])
            return carry

        _loop_by_pairs(n_chunks, logits_chunk, 0)
        m = jnp.max(m_ref[...], axis=1, keepdims=True)

        def pv_chunk(c, carry, m=m, kv_cols=kv_cols):
            off = chunk_start(c)
            p = jnp.exp(s_ref[:, pl.ds(off, kc)] - m)
            l_ref[...] = _lane_fold(p, jnp.add, l_ref[...])
            acc_ref[...] += _dot(p.astype(BF16), v_ref[pl.ds(off, kc), kv_cols])
            return carry

        _loop_by_pairs(n_chunks, pv_chunk, 0)
        out = acc_ref[...] / jnp.sum(l_ref[...], axis=1, keepdims=True)
        for r, h in enumerate(heads):
            o_ref[:, h * HEAD_DIM:(h + 1) * HEAD_DIM] = out[r * tq:(r + 1) * tq].astype(o_ref.dtype)


def _dsa_attention(q, qi, wi, k, v, ki, seq):
    t = q.shape[0]
    tq = DSA_TQ
    nts = seq // tq
    n_sel = min(TOPK_MAX, seq // 4)
    per_seq = lambda w: pl.BlockSpec((seq, w), lambda i: (i // nts, 0))
    return pl.pallas_call(
        functools.partial(_dsa_kernel, tiles_per_seq=nts, n_sel=n_sel),
        grid=(t // tq,),
        in_specs=[pl.BlockSpec((tq, MIX_WIDTH), lambda i: (i, 0)),
                  pl.BlockSpec((tq, IDX_HEADS * IDX_DIM), lambda i: (i, 0)),
                  pl.BlockSpec((tq, LANES), lambda i: (i, 0)),
                  per_seq(KV_WIDTH), per_seq(KV_WIDTH), per_seq(LANES)],
        out_specs=pl.BlockSpec((tq, MIX_WIDTH), lambda i: (i, 0)),
        out_shape=jax.ShapeDtypeStruct((t, MIX_WIDTH), BF16),
        scratch_shapes=[pltpu.VMEM((tq, seq), F32), pltpu.VMEM((seq, tq), F32),
                        pltpu.VMEM((HEADS_PER_KV * tq, seq), F32),
                        pltpu.VMEM((HEADS_PER_KV * tq, LANES), F32),
                        pltpu.VMEM((HEADS_PER_KV * tq, LANES), F32),
                        pltpu.VMEM((HEADS_PER_KV * tq, HEAD_DIM), F32)],
        compiler_params=_cparams("parallel"),
        name="dsa_attention",
    )(q, qi, wi, k, v, ki)


def _mem_attn_kernel(q_ref, k_ref, v_ref, o_ref):
    q = q_ref[...]
    for h in range(MEM_HEADS):
        sl = slice(h * HEAD_DIM, (h + 1) * HEAD_DIM)
        logits = _dot_nt(q[:, sl], k_ref[:, sl])
        e = jnp.exp(logits - jnp.max(logits, axis=-1, keepdims=True))
        p = e / jnp.sum(e, axis=-1, keepdims=True)
        o_ref[:, sl] = _dot(p.astype(BF16), v_ref[:, sl]).astype(o_ref.dtype)


def _mem_attention(qm, kv_mem, seq, mem_len):
    t = qm.shape[0]
    tq = MEM_TQ
    nqs = seq // tq
    return pl.pallas_call(
        _mem_attn_kernel,
        grid=(t // tq,),
        in_specs=[pl.BlockSpec((tq, MEM_WIDTH), lambda i: (i, 0)),
                  pl.BlockSpec((mem_len, MEM_WIDTH), lambda i: (i // nqs, 0)),
                  pl.BlockSpec((mem_len, MEM_WIDTH), lambda i: (i // nqs, 1))],
        out_specs=pl.BlockSpec((tq, MEM_WIDTH), lambda i: (i, 0)),
        out_shape=jax.ShapeDtypeStruct((t, MEM_WIDTH), BF16),
        compiler_params=_cparams("parallel"),
        name="mem_attention",
    )(qm, kv_mem, kv_mem)


def _outproj_ln_kernel(x_ref, mix_ref, mo_ref, w_ref, g_ref, b_ref, o_ref, wbf_ref):
    @pl.when(pl.program_id(0) == 0)
    def _():
        wbf_ref[...] = w_ref[...].astype(BF16)

    y = _dot(mix_ref[...], wbf_ref[0:MIX_WIDTH, :]) + _dot(mo_ref[...], wbf_ref[MIX_WIDTH:, :])
    o_ref[...] = _layer_norm(ALPHA * x_ref[...] + y, g_ref[...], b_ref[...])


def _outproj_ln(x, mix, mem_out, w_out, ln_g, ln_b, layer):
    t = x.shape[0]
    tm = LN_TM
    vec = lambda: pl.BlockSpec((None, 1, D_MODEL), lambda i: (layer, 0, 0))
    return pl.pallas_call(
        _outproj_ln_kernel,
        grid=(t // tm,),
        in_specs=[pl.BlockSpec((tm, D_MODEL), lambda i: (i, 0)),
                  pl.BlockSpec((tm, MIX_WIDTH), lambda i: (i, 0)),
                  pl.BlockSpec((tm, MEM_WIDTH), lambda i: (i, 0)),
                  pl.BlockSpec((None, D_MODEL, D_MODEL), lambda i: (layer, 0, 0)),
                  vec(), vec()],
        out_specs=pl.BlockSpec((tm, D_MODEL), lambda i: (i, 0)),
        out_shape=jax.ShapeDtypeStruct((t, D_MODEL), F32),
        scratch_shapes=[pltpu.VMEM((D_MODEL, D_MODEL), BF16)],
        compiler_params=_cparams("arbitrary"),
        name="outproj_ln",
    )(x, mix, mem_out, w_out, ln_g.reshape(DEPTH, 1, D_MODEL), ln_b.reshape(DEPTH, 1, D_MODEL))


def _router_kernel(x_ref, w_ref, b_ref, tri_ref, idx_ref, gate_ref, rank_ref, count_ref, seen_ref):
    @pl.when(pl.program_id(0) == 0)
    def _():
        seen_ref[...] = jnp.zeros(seen_ref.shape, F32)

    x = x_ref[...]
    w = w_ref[...]
    x_hi = x.astype(BF16)
    x_lo = (x - x_hi.astype(F32)).astype(BF16)
    w_hi = w.astype(BF16)
    w_lo = (w - w_hi.astype(F32)).astype(BF16)
    hi = _dot(x_hi, jnp.concatenate([w_hi, w_lo], axis=1))
    logits = (hi[:, :N_EXPERTS] + (hi[:, N_EXPERTS:] + _dot(x_lo, w_hi))) + b_ref[...]
    lane = lax.broadcasted_iota(I32, logits.shape, 1).astype(F32)
    vals, idxs = [], []
    for _ in range(TOP_K):
        m = jnp.max(logits, axis=-1, keepdims=True)
        ix = jnp.min(jnp.where(logits == m, lane, float(N_EXPERTS)), axis=-1, keepdims=True)
        vals.append(m)
        idxs.append(ix)
        logits = jnp.where(lane == ix, -jnp.inf, logits)
    e = jnp.exp(jnp.concatenate(vals, axis=1) - vals[0])
    idx_ref[...] = jnp.concatenate(idxs, axis=1).astype(I32)
    gate_ref[...] = e / jnp.sum(e, axis=-1, keepdims=True)
    hits = [jnp.where(lane == ix, 1.0, 0.0) for ix in idxs]
    routed = (hits[0] + hits[1]) + (hits[2] + hits[3])
    before = _dot(tri_ref[...], routed.astype(BF16)) + seen_ref[...]
    rank_ref[...] = jnp.concatenate([jnp.sum(hit * before, axis=-1, keepdims=True) for hit in hits],
                                    axis=1).astype(I32)
    seen_ref[...] += jnp.sum(routed, axis=0, keepdims=True)
    count_ref[...] = seen_ref[...].astype(I32)


def _router(x, router_w, router_b, layer):
    t = x.shape[0]
    tm = ROUTER_TM
    tri = jnp.tri(tm, k=-1, dtype=BF16)
    per_token = lambda: pl.BlockSpec((tm, TOP_K), lambda i: (i, 0))
    return pl.pallas_call(
        _router_kernel,
        grid=(t // tm,),
        in_specs=[pl.BlockSpec((tm, D_MODEL), lambda i: (i, 0)),
                  pl.BlockSpec((None, D_MODEL, N_EXPERTS), lambda i: (layer, 0, 0)),
                  pl.BlockSpec((None, 1, N_EXPERTS), lambda i: (layer, 0, 0)),
                  pl.BlockSpec((tm, tm), lambda i: (0, 0))],
        out_specs=[per_token(), per_token(), per_token(), pl.BlockSpec((1, N_EXPERTS), lambda i: (0, 0))],
        out_shape=[jax.ShapeDtypeStruct((t, TOP_K), I32), jax.ShapeDtypeStruct((t, TOP_K), F32),
                   jax.ShapeDtypeStruct((t, TOP_K), I32), jax.ShapeDtypeStruct((1, N_EXPERTS), I32)],
        scratch_shapes=[pltpu.VMEM((1, N_EXPERTS), F32)],
        compiler_params=_cparams("arbitrary"),
        name="router",
    )(x, router_w, router_b.reshape(DEPTH, 1, N_EXPERTS), tri)


def _dispatch_plan(top_idx, rank, counts):
    t = top_idx.shape[0]
    n_assign = t * TOP_K
    tm = MOE_TM
    n_tiles = n_assign // tm + N_EXPERTS + 1
    e = top_idx.reshape(n_assign)
    experts = jnp.arange(N_EXPERTS, dtype=I32)
    counts = counts.reshape(N_EXPERTS)
    padded = ((counts + tm - 1) // tm) * tm
    group_end = jnp.cumsum(padded)
    group_start = group_end - padded
    slot = rank.reshape(n_assign) + jnp.sum(jnp.where(e[:, None] == experts[None, :], group_start[None, :], 0), axis=1)
    n_used = group_end[-1] // tm
    tile = jnp.arange(n_tiles, dtype=I32)
    tile_expert = jnp.minimum(jnp.sum((tile[:, None] * tm >= group_end[None, :]).astype(I32), axis=1), N_EXPERTS - 1)
    last_expert = jnp.sum(jnp.where(tile == n_used - 1, tile_expert, 0))
    tile_expert = jnp.where(tile < n_used, tile_expert, last_expert).astype(I32)
    stride = 2 * n_assign
    pad = jnp.arange(tm, dtype=I32)
    pad_keys = jnp.where(pad[None, :] < (padded - counts)[:, None],
                         experts[:, None] * stride + n_assign + pad[None, :], jnp.iinfo(I32).max)
    keys = jnp.sort(jnp.concatenate([e * stride + jnp.arange(n_assign, dtype=I32), pad_keys.reshape(-1)]))
    index = keys % stride
    row_ids = jnp.where(index < n_assign, index // TOP_K, 0)
    row_ids = jnp.concatenate([row_ids, jnp.zeros(((n_tiles - n_assign // tm - N_EXPERTS) * tm,), I32)])
    nonempty = counts > 0
    later = jnp.where(nonempty[None, :] & (experts[None, :] > experts[:, None]), experts[None, :], N_EXPERTS)
    next_expert = jnp.min(later, axis=1)
    next_expert = jnp.where(next_expert < N_EXPERTS, next_expert, -1)
    group_parity = (jnp.cumsum(nonempty.astype(I32)) - 1) % 2
    info = jnp.stack([tile_expert, next_expert[tile_expert], group_parity[tile_expert]]).astype(I32)
    return row_ids.reshape(n_tiles, 1, tm), info, n_used.reshape(1).astype(I32), slot.reshape(t, TOP_K)


def _gather_rows(src_hbm, ids_ref, n_rows, dst_ref, sem):
    def body(r, carry):
        pltpu.make_async_copy(src_hbm.at[pl.ds(ids_ref[0, 0, r], 1)], dst_ref.at[pl.ds(r, 1)], sem).start()
        return carry
    lax.fori_loop(0, n_rows, body, 0, unroll=8)


def _wait_rows(dst_ref, sem):
    pltpu.make_async_copy(dst_ref, dst_ref, sem).wait()


def _moe_kernel(info_ref, nu_ref, ids_ref, ids_1_ref, ids_2_ref, x_hbm, wgu_hbm, bgu_ref, wd_hbm, bd_ref, o_ref,
                xbuf0, xbuf1, xbuf2, sems, wgu_f32, wd_f32, wsems, wgu_bf, wd_bf, *, layer):
    j = pl.program_id(0)
    n_used = nu_ref[0]
    tm = xbuf0.shape[0]
    bufs = ((xbuf0, sems.at[0]), (xbuf1, sems.at[1]), (xbuf2, sems.at[2]))
    expert, next_expert, wslot = info_ref[0, j], info_ref[1, j], info_ref[2, j]

    def weight_copies(e, slot):
        return (pltpu.make_async_copy(wgu_hbm.at[layer, e], wgu_f32.at[slot], wsems.at[0, slot]),
                pltpu.make_async_copy(wd_hbm.at[layer, e], wd_f32.at[slot], wsems.at[1, slot]))

    @pl.when(j == 0)
    def _():
        for cp in weight_copies(expert, wslot):
            cp.start()
        _gather_rows(x_hbm, ids_ref, tm, xbuf0, sems.at[0])
        _gather_rows(x_hbm, ids_1_ref, tm, xbuf1, sems.at[1])

    def expert_tile(cur, nxt):
        _wait_rows(*cur)

        @pl.when(jnp.logical_or(j == 0, expert != info_ref[0, jnp.maximum(j - 1, 0)]))
        def _():
            for cp in weight_copies(expert, wslot):
                cp.wait()

            @pl.when(next_expert >= 0)
            def _():
                for cp in weight_copies(next_expert, 1 - wslot):
                    cp.start()

            wgu_bf[...] = wgu_f32[wslot].astype(BF16)
            wd_bf[...] = wd_f32[wslot].astype(BF16)

        for r in range(tm):
            pltpu.make_async_copy(x_hbm.at[pl.ds(ids_2_ref[0, 0, r], 1)], nxt[0].at[pl.ds(r, 1)],
                                  nxt[1]).start(priority=r % 2)
        h = _dot(cur[0][...].astype(BF16), wgu_bf[...]) + bgu_ref[...]
        gate = jnp.minimum(h[:, :D_EXPERT], SWIGLU_LIMIT)
        up = jnp.clip(h[:, D_EXPERT:], -SWIGLU_LIMIT, SWIGLU_LIMIT)
        act = (up + 1.0) * gate * jax.nn.sigmoid(SWIGLU_ALPHA * gate)
        o_ref[...] = _dot(act.astype(BF16), wd_bf[...]) + bd_ref[...]

    for m in range(3):
        @pl.when(jnp.logical_and(j < n_used, j % 3 == m))
        def _(m=m):
            expert_tile(bufs[m], bufs[(m + 2) % 3])

        @pl.when(jnp.logical_and(jnp.logical_and(j >= n_used, j <= n_used + 1), j % 3 == m))
        def _(m=m):
            _wait_rows(*bufs[m])

    @pl.when(j >= n_used)
    def _():
        o_ref[...] = jnp.zeros_like(o_ref)


def _moe_experts(x, row_ids, tile_info, n_used, w_gate_up, b_gate_up, w_down, b_down, layer):
    n_tiles = row_ids.shape[0]
    tm = MOE_TM
    grid_spec = pltpu.PrefetchScalarGridSpec(
        num_scalar_prefetch=2,
        grid=(n_tiles,),
        in_specs=[pl.BlockSpec((1, 1, tm), lambda j, info, nu: (j, 0, 0), memory_space=pltpu.SMEM),
                  pl.BlockSpec((1, 1, tm), lambda j, info, nu: (jnp.minimum(j + 1, n_tiles - 1), 0, 0),
                               memory_space=pltpu.SMEM),
                  pl.BlockSpec((1, 1, tm), lambda j, info, nu: (jnp.minimum(j + 2, n_tiles - 1), 0, 0),
                               memory_space=pltpu.SMEM),
                  pl.BlockSpec(memory_space=pl.ANY),
                  pl.BlockSpec(memory_space=pl.ANY),
                  pl.BlockSpec((None, None, 1, 2 * D_EXPERT), lambda j, info, nu: (layer, info[0, j], 0, 0)),
                  pl.BlockSpec(memory_space=pl.ANY),
                  pl.BlockSpec((None, None, 1, D_MODEL), lambda j, info, nu: (layer, info[0, j], 0, 0))],
        out_specs=pl.BlockSpec((tm, D_MODEL), lambda j, info, nu: (j, 0)),
        scratch_shapes=[pltpu.VMEM((tm, D_MODEL), F32), pltpu.VMEM((tm, D_MODEL), F32), pltpu.VMEM((tm, D_MODEL), F32),
                        pltpu.SemaphoreType.DMA((3,)),
                        pltpu.VMEM((2, D_MODEL, 2 * D_EXPERT), F32), pltpu.VMEM((2, D_EXPERT, D_MODEL), F32),
                        pltpu.SemaphoreType.DMA((2, 2)),
                        pltpu.VMEM((D_MODEL, 2 * D_EXPERT), BF16), pltpu.VMEM((D_EXPERT, D_MODEL), BF16)],
    )
    return pl.pallas_call(
        functools.partial(_moe_kernel, layer=layer),
        grid_spec=grid_spec,
        out_shape=jax.ShapeDtypeStruct((n_tiles * tm, D_MODEL), F32),
        compiler_params=_cparams("arbitrary"),
        name="moe_experts",
    )(tile_info, n_used, row_ids, row_ids, row_ids, x, w_gate_up,
      b_gate_up.reshape(DEPTH, N_EXPERTS, 1, 2 * D_EXPERT), w_down, b_down.reshape(DEPTH, N_EXPERTS, 1, D_MODEL))


def _combine_ln_kernel(slot_ref, slot_1_ref, slot_2_ref, ys_hbm, gate_ref, x_ref, g_ref, b_ref, o_ref,
                       buf0, buf1, buf2, sems, *, n_steps):
    i = pl.program_id(0)
    rows = buf0.shape[0]
    tc = rows // TOP_K
    bufs = ((buf0, sems.at[0]), (buf1, sems.at[1]), (buf2, sems.at[2]))

    @pl.when(i == 0)
    def _():
        _gather_rows(ys_hbm, slot_ref, rows, buf0, sems.at[0])
        _gather_rows(ys_hbm, slot_1_ref, rows, buf1, sems.at[1])

    def token_tile(cur, nxt):
        _wait_rows(*cur)
        if nxt is not None:
            for r in range(rows):
                pltpu.make_async_copy(ys_hbm.at[pl.ds(slot_2_ref[0, 0, r], 1)], nxt[0].at[pl.ds(r, 1)],
                                      nxt[1]).start(priority=r % 2)
        gates = gate_ref[...]
        f = gates[:, 0:1] * cur[0][0:tc, :]
        for k in range(1, TOP_K):
            f = f + gates[:, k:k + 1] * cur[0][k * tc:(k + 1) * tc, :]
        o_ref[...] = _layer_norm(ALPHA * x_ref[...] + f, g_ref[...], b_ref[...])

    for m in range(3):
        @pl.when(jnp.logical_and(i < n_steps - 2, i % 3 == m))
        def _(m=m):
            token_tile(bufs[m], bufs[(m + 2) % 3])

    for last in (n_steps - 2, n_steps - 1):
        @pl.when(i == last)
        def _(last=last):
            token_tile(bufs[last % 3], None)


def _combine_ln(x, ys, slots, gates, ln_g, ln_b, layer):
    t = x.shape[0]
    tc = COMB_TC
    n = t // tc
    slots_km = slots.reshape(n, tc, TOP_K).transpose(0, 2, 1).reshape(n, 1, TOP_K * tc)
    vec = lambda: pl.BlockSpec((None, 1, D_MODEL), lambda i: (layer, 0, 0))
    return pl.pallas_call(
        functools.partial(_combine_ln_kernel, n_steps=n),
        grid=(n,),
        in_specs=[pl.BlockSpec((1, 1, TOP_K * tc), lambda i: (i, 0, 0), memory_space=pltpu.SMEM),
                  pl.BlockSpec((1, 1, TOP_K * tc), lambda i: (jnp.minimum(i + 1, n - 1), 0, 0),
                               memory_space=pltpu.SMEM),
                  pl.BlockSpec((1, 1, TOP_K * tc), lambda i: (jnp.minimum(i + 2, n - 1), 0, 0),
                               memory_space=pltpu.SMEM),
                  pl.BlockSpec(memory_space=pl.ANY),
                  pl.BlockSpec((tc, TOP_K), lambda i: (i, 0)),
                  pl.BlockSpec((tc, D_MODEL), lambda i: (i, 0)),
                  vec(), vec()],
        out_specs=pl.BlockSpec((tc, D_MODEL), lambda i: (i, 0)),
        out_shape=jax.ShapeDtypeStruct((t, D_MODEL), F32),
        scratch_shapes=[pltpu.VMEM((TOP_K * tc, D_MODEL), F32), pltpu.VMEM((TOP_K * tc, D_MODEL), F32),
                        pltpu.VMEM((TOP_K * tc, D_MODEL), F32), pltpu.SemaphoreType.DMA((3,))],
        compiler_params=_cparams("arbitrary"),
        name="combine_ln",
    )(slots_km, slots_km, slots_km, ys, gates, x, ln_g.reshape(DEPTH, 1, D_MODEL), ln_b.reshape(DEPTH, 1, D_MODEL))


def _rope_tables(seq):
    inv = ROPE_THETA ** (-jnp.arange(0, ROT_DIM, 2, dtype=F32) / ROT_DIM)
    ang = jnp.arange(seq, dtype=F32)[:, None] * inv[None, :]
    cos, sin = jnp.cos(ang), jnp.sin(ang)
    half = ROT_DIM // 2
    rest = HEAD_DIM - ROT_DIM
    zeros_h = jnp.zeros((seq, half), F32)
    head = lambda a, b, fill: jnp.concatenate([a, b, jnp.full((seq, rest), fill, F32)], axis=1)
    tab = jnp.stack([head(cos, cos, 1.0), head(zeros_h, sin, 0.0), head(-sin, zeros_h, 0.0)])
    return jnp.tile(tab, (1, 1, LANES // HEAD_DIM))


def kernel(x, mem, a_w_in, a_conv_w, b_w_in, b_q_norm_g, b_w_uq, b_w_uq_idx, c_w_in, c_sinks, w_out, w_kv_mem,
           ln1_g, ln1_b, router_w, router_b, w_gate_up, b_gate_up, w_down, b_down, ln2_g, ln2_b):
    batch, seq, _ = x.shape
    mem_len = mem.shape[1]
    t = batch * seq
    xs = x.reshape(t, D_MODEL)
    mem2 = mem.reshape(batch * mem_len, D_MODEL)
    rope_tab = _rope_tables(seq)

    o1 = Q_RANK
    o3 = o1 + 2 * KV_WIDTH
    o5 = o3 + IDX_DIM + IDX_HEADS
    b_w = jnp.concatenate([b_w_in[:, :, :o3], b_w_in[:, :, o5:], b_w_in[:, :, o3:o5],
                           jnp.zeros(b_w_in.shape[:2] + (LANES - IDX_DIM - IDX_HEADS,), b_w_in.dtype)], axis=2)
    a_w, b_w, c_w = a_w_in.astype(BF16), b_w.astype(BF16), c_w_in.astype(BF16)
    b_wq, b_wqi = b_w_uq.astype(BF16), b_w_uq_idx.astype(BF16)

    for i in range(DEPTH):
        kind, j = i % N_MIXERS, i // N_MIXERS
        if kind == 0:
            mix, qm = _conv_mixer(xs, a_w, a_conv_w, j, seq)
        elif kind == 1:
            q, qi, k, v, ki, wi, qm = _dsa_proj(xs, b_w, b_q_norm_g, b_wq, b_wqi, j, rope_tab, seq)
            mix = _dsa_attention(q, qi, wi, k, v, ki, seq)
        else:
            q, k, v, qm = _swa_proj(xs, c_w, j, rope_tab, seq)
            mix = _swa_mixer(q, k, v, c_sinks, j, seq)
        kv_mem = _matmul(mem2, w_kv_mem, i, BF16, min(MM_TM, batch * mem_len), MM_TN)
        mem_out = _mem_attention(qm, kv_mem, seq, mem_len)
        xs = _outproj_ln(xs, mix, mem_out, w_out, ln1_g, ln1_b, i)
        top_idx, gates, rank, counts = _router(xs, router_w, router_b, i)
        row_ids, tile_info, n_used, slots = _dispatch_plan(top_idx, rank, counts)
        ys = _moe_experts(xs, row_ids, tile_info, n_used, w_gate_up, b_gate_up, w_down, b_down, i)
        xs = _combine_ln(xs, ys, slots, gates, ln2_g, ln2_b, i)
    return xs.reshape(batch, seq, D_MODEL)
```
